```python
import jax, jax.numpy as jnp
from jax import lax
import numpy as np

D_MODEL = 2048
BATCH = 16
SEQ = 2048
DEPTH = 4

D_FF = 4 * D_MODEL
CONV_WIDTH = 3
N_HEADS = 16
QK_HEAD_DIM = 128
V_HEAD_DIM = 128
Q_LORA = 512
KV_LORA = 256
IDX_HEADS = 16
IDX_DIM = 64
TOPK_MAX = 256
Q_BLOCK = 64
ATTN_SCALE = QK_HEAD_DIM ** -0.5
IN_COLS = Q_LORA + KV_LORA + IDX_DIM + IDX_HEADS
EPS = 1e-6
N_CONV = (DEPTH + 1) // 2
N_ATTN = DEPTH // 2

kernel_name = "hybrid_shortconv_dsa_sqrelu"


def rmsnorm(x, g):
    xf = x.astype(jnp.float32)
    y = xf * lax.rsqrt(jnp.mean(xf * xf, axis=-1, keepdims=True) + EPS)
    return (y * g.astype(jnp.float32)).astype(x.dtype)


def layernorm(x, g, b):
    xf = x.astype(jnp.float32)
    mu = jnp.mean(xf, axis=-1, keepdims=True)
    var = jnp.mean(jnp.square(xf - mu), axis=-1, keepdims=True)
    y = (xf - mu) * lax.rsqrt(var + EPS)
    return y * g.astype(jnp.float32) + b.astype(jnp.float32)


def sq_relu_mlp(h, w1, w2):
    return jnp.square(jax.nn.relu(h @ w1)) @ w2


def short_conv_mixer(h, w_in, conv_w, w_out):
    bg, cg, xt = jnp.split(h @ w_in, 3, axis=-1)
    z = cg * xt
    zp = jnp.pad(z, ((0, 0), (CONV_WIDTH - 1, 0), (0, 0)))
    s = z.shape[1]
    zc = conv_w[0] * zp[:, 0:s] + conv_w[1] * zp[:, 1:s + 1] + conv_w[2] * zp[:, 2:s + 2]
    return (bg * zc) @ w_out


def dsa_mixer(h, w_in, q_g, kv_g, w_uq, w_uk, w_uv, w_qidx, ln_g, ln_b, w_out):
    b, s, _ = h.shape
    topk = min(TOPK_MAX, s // 4)
    proj = h @ w_in
    c_q, c_kv, k_idx, w_idx = jnp.split(
        proj, [Q_LORA, Q_LORA + KV_LORA, Q_LORA + KV_LORA + IDX_DIM], axis=-1)
    c_q = rmsnorm(c_q, q_g)
    c_kv = rmsnorm(c_kv, kv_g)
    k_idx = layernorm(k_idx, ln_g, ln_b)
    w_idx = w_idx.astype(jnp.float32) * (IDX_HEADS ** -0.5 * IDX_DIM ** -0.5)
    nb = s // Q_BLOCK
    key_pos = jnp.arange(s, dtype=jnp.int32)

    def to_blocks(a):
        return a.reshape(b, nb, Q_BLOCK, *a.shape[2:]).swapaxes(0, 1)

    def block(args):
        cq, wi, t = args
        q = jnp.einsum('bqr,rhd->bqhd', cq, w_uq)
        q_lat = jnp.einsum('bqhd,hdc->bqhc', q, w_uk)
        q_idx = jnp.einsum('bqr,rhe->bqhe', cq, w_qidx).astype(jnp.float32)
        rel = jax.nn.relu(jnp.einsum('bqhe,bse->bqhs', q_idx, k_idx))
        score = jnp.einsum('bqhs,bqh->bqs', rel, wi)
        causal = key_pos[None, :] <= t[:, None]
        score = jnp.where(causal[None], score, -jnp.inf)
        _, idx = lax.top_k(score, topk)
        kv_sel = jax.vmap(lambda c, i: c[i])(c_kv, idx)
        logits = jnp.einsum('bqhc,bqkc->bqhk', q_lat, kv_sel).astype(jnp.float32) * ATTN_SCALE
        valid = (idx <= t[None, :, None])[:, :, None, :]
        p = jax.nn.softmax(jnp.where(valid, logits, -jnp.inf), axis=-1).astype(h.dtype)
        o_lat = jnp.einsum('bqhk,bqkc->bqhc', p, kv_sel)
        o = jnp.einsum('bqhc,hcv->bqhv', o_lat, w_uv).reshape(b, Q_BLOCK, N_HEADS * V_HEAD_DIM)
        return o @ w_out

    out = lax.map(block, (to_blocks(c_q), to_blocks(w_idx), key_pos.reshape(nb, Q_BLOCK)))
    return out.swapaxes(0, 1).reshape(b, s, -1)


def setup_inputs(seed: int = 0) -> dict:
    key = jax.random.key(seed)
    ks = jax.random.split(key, 20)
    f32 = jnp.float32

    def nrm(k, shape, fan_in):
        return jax.random.normal(k, shape, f32) * (fan_in ** -0.5)

    def gain(k, shape):
        return 1.0 + 0.02 * jax.random.normal(k, shape, f32)

    return {
        "x": jax.random.normal(ks[0], (BATCH, SEQ, D_MODEL), f32),
        "norm_mix": gain(ks[1], (DEPTH, D_MODEL)),
        "norm_mlp": gain(ks[2], (DEPTH, D_MODEL)),
        "mlp_w1": nrm(ks[3], (DEPTH, D_MODEL, D_FF), D_MODEL),
        "mlp_w2": nrm(ks[4], (DEPTH, D_FF, D_MODEL), D_FF),
        "conv_in": nrm(ks[5], (N_CONV, D_MODEL, 3 * D_MODEL), D_MODEL),
        "conv_w": nrm(ks[6], (N_CONV, CONV_WIDTH, D_MODEL), CONV_WIDTH),
        "conv_out": nrm(ks[7], (N_CONV, D_MODEL, D_MODEL), D_MODEL),
        "attn_in": nrm(ks[8], (N_ATTN, D_MODEL, IN_COLS), D_MODEL),
        "q_norm": gain(ks[9], (N_ATTN, Q_LORA)),
        "kv_norm": gain(ks[10], (N_ATTN, KV_LORA)),
        "w_uq": nrm(ks[11], (N_ATTN, Q_LORA, N_HEADS, QK_HEAD_DIM), Q_LORA),
        "w_uk": nrm(ks[12], (N_ATTN, N_HEADS, QK_HEAD_DIM, KV_LORA), QK_HEAD_DIM),
        "w_uv": nrm(ks[13], (N_ATTN, N_HEADS, KV_LORA, V_HEAD_DIM), KV_LORA),
        "w_qidx": nrm(ks[14], (N_ATTN, Q_LORA, IDX_HEADS, IDX_DIM), Q_LORA),
        "kidx_ln_g": gain(ks[15], (N_ATTN, IDX_DIM)),
        "kidx_ln_b": 0.02 * jax.random.normal(ks[16], (N_ATTN, IDX_DIM), f32),
        "attn_out": nrm(ks[17], (N_ATTN, N_HEADS * V_HEAD_DIM, D_MODEL), N_HEADS * V_HEAD_DIM),
        "final_norm": gain(ks[18], (D_MODEL,)),
    }


def reference(x, norm_mix, norm_mlp, mlp_w1, mlp_w2, conv_in, conv_w, conv_out,
              attn_in, q_norm, kv_norm, w_uq, w_uk, w_uv, w_qidx, kidx_ln_g, kidx_ln_b,
              attn_out, final_norm):
    h = x
    for i in range(DEPTH):
        j = i // 2
        hn = rmsnorm(h, norm_mix[i])
        if i % 2 == 0:
            h = h + short_conv_mixer(hn, conv_in[j], conv_w[j], conv_out[j])
        else:
            h = h + dsa_mixer(hn, attn_in[j], q_norm[j], kv_norm[j], w_uq[j], w_uk[j],
                              w_uv[j], w_qidx[j], kidx_ln_g[j], kidx_ln_b[j], attn_out[j])
        h = h + sq_relu_mlp(rmsnorm(h, norm_mlp[i]), mlp_w1[i], mlp_w2[i])
    return rmsnorm(h, final_norm)
```

```python
import functools

import jax
import jax.numpy as jnp
from jax import lax
from jax.experimental import pallas as pl
from jax.experimental.pallas import tpu as pltpu

EPS = 1e-6
TOPK_MAX = 256
LANES = 128
VMEM_LIMIT_BYTES = 56 * 1024 * 1024
INT_MIN = -2 ** 31

F32 = jnp.float32
BF16 = jnp.bfloat16


def _params(n_axes):
    return pltpu.CompilerParams(
        dimension_semantics=("arbitrary",) * n_axes,
        vmem_limit_bytes=VMEM_LIMIT_BYTES)


def _rms(x, g):
    return x * lax.rsqrt(jnp.mean(x * x, axis=-1, keepdims=True) + EPS) * g


def _dot(a, b):
    return jnp.dot(a, b, preferred_element_type=F32)


def _mlp_kernel(h_ref, g_ref, w1_ref, w2_ref, gf_ref, o_ref, hn_ref, *, final):
    j = pl.program_id(1)

    @pl.when(j == 0)
    def _():
        h = h_ref[...]
        hn_ref[...] = _rms(h, g_ref[...]).astype(BF16)
        o_ref[...] = h

    a = jnp.maximum(_dot(hn_ref[...], w1_ref[...]), 0.0)
    o_ref[...] += _dot((a * a).astype(BF16), w2_ref[...])

    if final:
        @pl.when(j == pl.num_programs(1) - 1)
        def _():
            o_ref[...] = _rms(o_ref[...], gf_ref[...])


def _mlp(h, g, w1, w2, gf, *, final, tm, tf):
    n, d = h.shape
    ff = w1.shape[1]
    return pl.pallas_call(
        functools.partial(_mlp_kernel, final=final),
        grid=(n // tm, ff // tf),
        in_specs=[
            pl.BlockSpec((tm, d), lambda i, j: (i, 0)),
            pl.BlockSpec((1, d), lambda i, j: (0, 0)),
            pl.BlockSpec((d, tf), lambda i, j: (0, j)),
            pl.BlockSpec((tf, d), lambda i, j: (j, 0)),
            pl.BlockSpec((1, d), lambda i, j: (0, 0)),
        ],
        out_specs=pl.BlockSpec((tm, d), lambda i, j: (i, 0)),
        out_shape=jax.ShapeDtypeStruct((n, d), F32),
        scratch_shapes=[pltpu.VMEM((tm, d), BF16)],
        compiler_params=_params(2),
        name="sqrelu_mlp",
    )(h, g, w1, w2, gf)


def _conv_kernel(h_ref, g_ref, wb_ref, wc_ref, wx_ref, cw_ref, wo_ref, o_ref,
                 hn_ref, halo_ref, *, tiles_per_seq):
    i = pl.program_id(0)
    j = pl.program_id(1)

    @pl.when(j == 0)
    def _():
        h = h_ref[...]
        hn_ref[...] = _rms(h, g_ref[...]).astype(BF16)
        o_ref[...] = h

    hn = hn_ref[...]
    bg = _dot(hn, wb_ref[...])
    z = _dot(hn, wc_ref[...]) * _dot(hn, wx_ref[...])
    tm = z.shape[0]

    @pl.when(i % tiles_per_seq == 0)
    def _():
        halo_ref[j] = jnp.zeros(halo_ref.shape[1:], F32)

    prev = halo_ref[j]
    halo_ref[j] = z[tm - 8:, :]
    p1 = prev[7:8, :]
    p2 = prev[6:7, :]

    row = lax.broadcasted_iota(jnp.int32, (tm, 1), 0)
    z1 = jnp.where(row == 0, p1, pltpu.roll(z, 1, 0))
    z2 = jnp.where(row == 0, p2, jnp.where(row == 1, p1, pltpu.roll(z, 2, 0)))
    cw = cw_ref[...]
    zc = cw[0:1, :] * z2 + cw[1:2, :] * z1 + cw[2:3, :] * z
    o_ref[...] += _dot((bg * zc).astype(BF16), wo_ref[...])


def _conv(h, g, w_in, cw, w_out, *, seq, tm, tc):
    n, d = h.shape
    nc = d // tc
    return pl.pallas_call(
        functools.partial(_conv_kernel, tiles_per_seq=seq // tm),
        grid=(n // tm, nc),
        in_specs=[
            pl.BlockSpec((tm, d), lambda i, j: (i, 0)),
            pl.BlockSpec((1, d), lambda i, j: (0, 0)),
            pl.BlockSpec((d, tc), lambda i, j: (0, j)),
            pl.BlockSpec((d, tc), lambda i, j: (0, j + nc)),
            pl.BlockSpec((d, tc), lambda i, j: (0, j + 2 * nc)),
            pl.BlockSpec((8, tc), lambda i, j: (0, j)),
            pl.BlockSpec((tc, d), lambda i, j: (j, 0)),
        ],
        out_specs=pl.BlockSpec((tm, d), lambda i, j: (i, 0)),
        out_shape=jax.ShapeDtypeStruct((n, d), F32),
        scratch_shapes=[pltpu.VMEM((tm, d), BF16), pltpu.VMEM((nc, 8, tc), F32)],
        compiler_params=_params(2),
        name="short_conv",
    )(h, g, w_in, w_in, w_in, cw, w_out)


def _attn_proj_kernel(h_ref, g_ref, w_ref, qg_ref, kvg_ref, lng_ref, lnb_ref,
                      cq_ref, ckv_ref, ckvt_ref, kidxt_ref, widx_ref,
                      *, q_lora, kv_lora, idx_dim, w_scale):
    hn = _rms(h_ref[...], g_ref[...]).astype(BF16)
    proj = _dot(hn, w_ref[...])
    cq_ref[...] = _rms(proj[:, :q_lora], qg_ref[...]).astype(BF16)
    ckv = _rms(proj[:, q_lora:q_lora + kv_lora], kvg_ref[...])
    ckv_ref[...] = ckv.astype(BF16)
    ckvt_ref[...] = ckv.T.astype(BF16)

    o = q_lora + kv_lora
    kx = proj[:, o:o + LANES]
    valid = lax.broadcasted_iota(jnp.int32, (1, LANES), 1) < idx_dim
    mu = jnp.sum(kx, axis=-1, keepdims=True) * (1.0 / idx_dim)
    dlt = jnp.where(valid, kx - mu, 0.0)
    var = jnp.sum(dlt * dlt, axis=-1, keepdims=True) * (1.0 / idx_dim)
    kn = dlt * lax.rsqrt(var + EPS) * lng_ref[...] + lnb_ref[...]
    kidxt_ref[...] = kn.T.astype(BF16)
    widx_ref[...] = proj[:, o + LANES:o + 2 * LANES] * w_scale


def _attn_proj(h, g, w, qg, kvg, lng, lnb, *, q_lora, kv_lora, idx_dim, w_scale, tm):
    n, d = h.shape
    cols = w.shape[1]
    row = lambda c: pl.BlockSpec((1, c), lambda i: (0, 0))
    return pl.pallas_call(
        functools.partial(_attn_proj_kernel, q_lora=q_lora, kv_lora=kv_lora,
                          idx_dim=idx_dim, w_scale=w_scale),
        grid=(n // tm,),
        in_specs=[
            pl.BlockSpec((tm, d), lambda i: (i, 0)),
            row(d),
            pl.BlockSpec((d, cols), lambda i: (0, 0)),
            row(q_lora), row(kv_lora), row(LANES), row(LANES),
        ],
        out_specs=[
            pl.BlockSpec((tm, q_lora), lambda i: (i, 0)),
            pl.BlockSpec((tm, kv_lora), lambda i: (i, 0)),
            pl.BlockSpec((kv_lora, tm), lambda i: (0, i)),
            pl.BlockSpec((LANES, tm), lambda i: (0, i)),
            pl.BlockSpec((tm, LANES), lambda i: (i, 0)),
        ],
        out_shape=[
            jax.ShapeDtypeStruct((n, q_lora), BF16),
            jax.ShapeDtypeStruct((n, kv_lora), BF16),
            jax.ShapeDtypeStruct((kv_lora, n), BF16),
            jax.ShapeDtypeStruct((LANES, n), BF16),
            jax.ShapeDtypeStruct((n, LANES), F32),
        ],
        compiler_params=_params(1),
        name="attn_proj",
    )(h, g, w, qg, kvg, lng, lnb)


def _attn_kernel(cq_ref, widx_ref, kidxt_ref, ckvt_ref, ckv_ref,
                 wuq_ref, wuk_ref, wqi_ref, wuv_ref, o_ref,
                 q_s, qi_s, w_s, key_s, bias_s, o_s,
                 *, n_heads, topk, scale):
    tq, s = key_s.shape
    hd = q_s.shape[2]
    t0 = pl.program_id(1) * tq
    cq = cq_ref[...]

    q_all = _dot(cq, wuq_ref[...]).astype(BF16)
    qi_all = _dot(cq, wqi_ref[...]).astype(BF16)
    widx = widx_ref[...]
    for h in range(n_heads):
        q_s[h] = q_all[:, h * hd:(h + 1) * hd]
        qi_s[h] = qi_all[:, h * LANES:(h + 1) * LANES]
        w_s[h] = jnp.broadcast_to(widx[:, h:h + 1], (tq, LANES))

    def score_body(h, acc):
        r = jnp.maximum(_dot(qi_s[h], kidxt_ref[...]), 0.0)
        return acc + w_s[h][:, 0:1] * r
    score = lax.fori_loop(0, n_heads, score_body, jnp.zeros((tq, s), F32))

    row = t0 + lax.broadcasted_iota(jnp.int32, (tq, 1), 0)
    col = lax.broadcasted_iota(jnp.int32, (1, s), 1)
    causal = col <= row
    bits = pltpu.bitcast(score, jnp.int32)
    key = bits ^ ((bits >> 31) & 0x7FFFFFFF)
    key_s[...] = jnp.where(causal, key, INT_MIN)

    def count(pred):
        return jnp.sum(pred.astype(F32), axis=1, keepdims=True)

    def select_body(i, thr_u):
        cand_u = thr_u | jnp.left_shift(jnp.int32(1), 31 - i)
        ok = count(key_s[...] >= (cand_u ^ INT_MIN)) >= topk
        return jnp.where(ok, cand_u, thr_u)
    thr_u = lax.fori_loop(0, 32, select_body, jnp.zeros((tq, 1), jnp.int32))
    thr = thr_u ^ INT_MIN

    key = key_s[...]
    ge = (key >= thr) & causal
    bias_s[...] = jnp.where(ge, 0.0, -jnp.inf)
    n_ge = count(ge)

    @pl.when(jnp.max(n_ge) > topk)
    def _():
        gt = (key > thr) & causal
        eq = (key == thr) & causal
        keep = topk - count(gt)

        def cut_body(i, c):
            cand = c | jnp.left_shift(jnp.int32(1), nbits - 1 - i)
            ok = count(eq & (col < cand)) <= keep
            return jnp.where(ok, cand, c)
        nbits = s.bit_length()
        cut = lax.fori_loop(0, nbits, cut_body, jnp.zeros((tq, 1), jnp.int32))
        bias_s[...] = jnp.where(gt | (eq & (col < cut)), 0.0, -jnp.inf)

    def head_body(h, carry):
        q_lat = _dot(q_s[h], wuk_ref[h]).astype(BF16)
        lg = _dot(q_lat, ckvt_ref[...]) * scale + bias_s[...]
        e = jnp.exp(lg - jnp.max(lg, axis=1, keepdims=True))
        den = jnp.sum(e, axis=1, keepdims=True)
        o_lat = _dot(e.astype(BF16), ckv_ref[...]) / den
        o_s[h] = _dot(o_lat.astype(BF16), wuv_ref[h]).astype(BF16)
        return carry
    lax.fori_loop(0, n_heads, head_body, 0)

    for h in range(n_heads):
        o_ref[:, h * o_s.shape[2]:(h + 1) * o_s.shape[2]] = o_s[h]


def _attn(cq, widx, kidxt, ckvt, ckv, wuq, wuk, wqi, wuv, *, batch, seq, topk, scale, tq):
    n, q_lora = cq.shape
    n_heads, hd, kv_lora = wuk.shape
    vd = wuv.shape[2]
    nq = seq // tq
    const2 = lambda a: pl.BlockSpec(a.shape, lambda b, q: (0, 0))
    const3 = lambda a: pl.BlockSpec(a.shape, lambda b, q: (0, 0, 0))
    return pl.pallas_call(
        functools.partial(_attn_kernel, n_heads=n_heads, topk=topk, scale=scale),
        grid=(batch, nq),
        in_specs=[
            pl.BlockSpec((tq, q_lora), lambda b, q: (b * nq + q, 0)),
            pl.BlockSpec((tq, LANES), lambda b, q: (b * nq + q, 0)),
            pl.BlockSpec((LANES, seq), lambda b, q: (0, b)),
            pl.BlockSpec((kv_lora, seq), lambda b, q: (0, b)),
            pl.BlockSpec((seq, kv_lora), lambda b, q: (b, 0)),
            const2(wuq), const3(wuk), const2(wqi), const3(wuv),
        ],
        out_specs=pl.BlockSpec((tq, n_heads * vd), lambda b, q: (b * nq + q, 0)),
        out_shape=jax.ShapeDtypeStruct((n, n_heads * vd), BF16),
        scratch_shapes=[
            pltpu.VMEM((n_heads, tq, hd), BF16),
            pltpu.VMEM((n_heads, tq, LANES), BF16),
            pltpu.VMEM((n_heads, tq, LANES), F32),
            pltpu.VMEM((tq, seq), jnp.int32),
            pltpu.VMEM((tq, seq), F32),
            pltpu.VMEM((n_heads, tq, vd), BF16),
        ],
        compiler_params=_params(2),
        name="dsa_attention",
    )(cq, widx, kidxt, ckvt, ckv, wuq, wuk, wqi, wuv)


def _out_proj_kernel(h_ref, o_ref, w_ref, out_ref):
    out_ref[...] = h_ref[...] + _dot(o_ref[...], w_ref[...])


def _out_proj(h, o, w, *, tm):
    n, d = h.shape
    k = o.shape[1]
    return pl.pallas_call(
        _out_proj_kernel,
        grid=(n // tm,),
        in_specs=[
            pl.BlockSpec((tm, d), lambda i: (i, 0)),
            pl.BlockSpec((tm, k), lambda i: (i, 0)),
            pl.BlockSpec((k, d), lambda i: (0, 0)),
        ],
        out_specs=pl.BlockSpec((tm, d), lambda i: (i, 0)),
        out_shape=jax.ShapeDtypeStruct((n, d), F32),
        compiler_params=_params(1),
        name="attn_out_proj",
    )(h, o, w)


def _pad_last(a, width):
    return jnp.pad(a, [(0, 0)] * (a.ndim - 1) + [(0, width - a.shape[-1])])


def _tile(n, want):
    t = min(n, want)
    assert n % t == 0, (n, t)
    return t


def kernel(x, norm_mix, norm_mlp, mlp_w1, mlp_w2, conv_in, conv_w, conv_out, attn_in,
           q_norm, kv_norm, w_uq, w_uk, w_uv, w_qidx, kidx_ln_g, kidx_ln_b, attn_out,
           final_norm):
    batch, seq, d = x.shape
    depth = norm_mix.shape[0]
    q_lora, n_heads, hd = w_uq.shape[1:]
    kv_lora = w_uk.shape[3]
    idx_heads, idx_dim = w_qidx.shape[2:]
    assert idx_heads == n_heads and idx_dim <= LANES and n_heads <= LANES
    topk = min(TOPK_MAX, seq // 4)
    n = batch * seq

    tm = _tile(seq, 512)
    tf = _tile(mlp_w1.shape[2], 1024)
    tc = _tile(d, 512)
    tq = _tile(seq, 256)

    h = x.reshape(n, d)
    row = lambda v: v.reshape(1, -1)
    gf = row(final_norm)
    for i in range(depth):
        j = i // 2
        g = row(norm_mix[i])
        if i % 2 == 0:
            cw = jnp.pad(conv_w[j], ((0, 8 - conv_w.shape[1]), (0, 0)))
            h = _conv(h, g, conv_in[j].astype(BF16), cw, conv_out[j].astype(BF16),
                      seq=seq, tm=tm, tc=tc)
        else:
            o = q_lora + kv_lora
            w = jnp.concatenate([
                attn_in[j][:, :o],
                _pad_last(attn_in[j][:, o:o + idx_dim], LANES),
                _pad_last(attn_in[j][:, o + idx_dim:], LANES)], axis=1).astype(BF16)
            cq, ckv, ckvt, kidxt, widx = _attn_proj(
                h, g, w, row(q_norm[j]), row(kv_norm[j]),
                _pad_last(row(kidx_ln_g[j]), LANES), _pad_last(row(kidx_ln_b[j]), LANES),
                q_lora=q_lora, kv_lora=kv_lora, idx_dim=idx_dim,
                w_scale=float(idx_heads ** -0.5 * idx_dim ** -0.5), tm=tm)
            wqi = _pad_last(w_qidx[j], LANES).reshape(q_lora, n_heads * LANES)
            oh = _attn(cq, widx, kidxt, ckvt, ckv,
                       w_uq[j].reshape(q_lora, n_heads * hd).astype(BF16),
                       w_uk[j].astype(BF16), wqi.astype(BF16), w_uv[j].astype(BF16),
                       batch=batch, seq=seq, topk=topk, scale=float(hd ** -0.5), tq=tq)
            h = _out_proj(h, oh, attn_out[j].astype(BF16), tm=tm)
        h = _mlp(h, row(norm_mlp[i]), mlp_w1[i].astype(BF16), mlp_w2[i].astype(BF16), gf,
                 final=(i == depth - 1), tm=tm, tf=tf)
    return h.reshape(batch, seq, d)
```

```python
import functools

import jax
import jax.numpy as jnp
from jax import lax
from jax.experimental import pallas as pl
from jax.experimental.pallas import tpu as pltpu

EPS = 1e-6
TOPK_MAX = 256
LANES = 128
VMEM_LIMIT_BYTES = 56 * 1024 * 1024
INT_MIN = -2 ** 31
LOG2_E = 1.4426950408889634
SCORE_GROUP = 4

F32 = jnp.float32
BF16 = jnp.bfloat16


def _params(n_axes):
    return pltpu.CompilerParams(
        dimension_semantics=("arbitrary",) * n_axes,
        vmem_limit_bytes=VMEM_LIMIT_BYTES)


def _rms(x, g):
    return x * lax.rsqrt(jnp.mean(x * x, axis=-1, keepdims=True) + EPS) * g


def _dot(a, b):
    return jnp.dot(a, b, preferred_element_type=F32)


def _mlp_kernel(h_ref, g_ref, w1_ref, w2_ref, gf_ref, o_ref, hn_ref, *, final):
    j = pl.program_id(1)

    @pl.when(j == 0)
    def _():
        h = h_ref[...]
        hn_ref[...] = _rms(h, g_ref[...]).astype(BF16)
        o_ref[...] = h

    a = jnp.maximum(_dot(hn_ref[...], w1_ref[...]), 0.0)
    o_ref[...] += _dot((a * a).astype(BF16), w2_ref[...])

    if final:
        @pl.when(j == pl.num_programs(1) - 1)
        def _():
            o_ref[...] = _rms(o_ref[...], gf_ref[...])


def _mlp(h, g, w1, w2, gf, *, final, tm, tf):
    n, d = h.shape
    ff = w1.shape[1]
    return pl.pallas_call(
        functools.partial(_mlp_kernel, final=final),
        grid=(n // tm, ff // tf),
        in_specs=[
            pl.BlockSpec((tm, d), lambda i, j: (i, 0)),
            pl.BlockSpec((1, d), lambda i, j: (0, 0)),
            pl.BlockSpec((d, tf), lambda i, j: (0, j)),
            pl.BlockSpec((tf, d), lambda i, j: (j, 0)),
            pl.BlockSpec((1, d), lambda i, j: (0, 0)),
        ],
        out_specs=pl.BlockSpec((tm, d), lambda i, j: (i, 0)),
        out_shape=jax.ShapeDtypeStruct((n, d), F32),
        scratch_shapes=[pltpu.VMEM((tm, d), BF16)],
        compiler_params=_params(2),
        name="sqrelu_mlp",
    )(h, g, w1, w2, gf)


def _conv_kernel(h_ref, g_ref, wb_ref, wc_ref, wx_ref, cw_ref, wo_ref, o_ref,
                 hn_ref, halo_ref, *, tiles_per_seq):
    i = pl.program_id(0)
    j = pl.program_id(1)

    @pl.when(j == 0)
    def _():
        h = h_ref[...]
        hn_ref[...] = _rms(h, g_ref[...]).astype(BF16)
        o_ref[...] = h

    hn = hn_ref[...]
    bg = _dot(hn, wb_ref[...])
    z = _dot(hn, wc_ref[...]) * _dot(hn, wx_ref[...])
    tm = z.shape[0]

    @pl.when(i % tiles_per_seq == 0)
    def _():
        halo_ref[j] = jnp.zeros(halo_ref.shape[1:], F32)

    prev = halo_ref[j]
    halo_ref[j] = z[tm - 8:, :]
    p1 = prev[7:8, :]
    p2 = prev[6:7, :]

    row = lax.broadcasted_iota(jnp.int32, (tm, 1), 0)
    z1 = jnp.where(row == 0, p1, pltpu.roll(z, 1, 0))
    z2 = jnp.where(row == 0, p2, jnp.where(row == 1, p1, pltpu.roll(z, 2, 0)))
    cw = cw_ref[...]
    zc = cw[0:1, :] * z2 + cw[1:2, :] * z1 + cw[2:3, :] * z
    o_ref[...] += _dot((bg * zc).astype(BF16), wo_ref[...])


def _conv(h, g, w_in, cw, w_out, *, seq, tm, tc):
    n, d = h.shape
    nc = d // tc
    return pl.pallas_call(
        functools.partial(_conv_kernel, tiles_per_seq=seq // tm),
        grid=(n // tm, nc),
        in_specs=[
            pl.BlockSpec((tm, d), lambda i, j: (i, 0)),
            pl.BlockSpec((1, d), lambda i, j: (0, 0)),
            pl.BlockSpec((d, tc), lambda i, j: (0, j)),
            pl.BlockSpec((d, tc), lambda i, j: (0, j + nc)),
            pl.BlockSpec((d, tc), lambda i, j: (0, j + 2 * nc)),
            pl.BlockSpec((8, tc), lambda i, j: (0, j)),
            pl.BlockSpec((tc, d), lambda i, j: (j, 0)),
        ],
        out_specs=pl.BlockSpec((tm, d), lambda i, j: (i, 0)),
        out_shape=jax.ShapeDtypeStruct((n, d), F32),
        scratch_shapes=[pltpu.VMEM((tm, d), BF16), pltpu.VMEM((nc, 8, tc), F32)],
        compiler_params=_params(2),
        name="short_conv",
    )(h, g, w_in, w_in, w_in, cw, w_out)


def _attn_proj_kernel(h_ref, g_ref, w_ref, qg_ref, kvg_ref, lng_ref, lnb_ref,
                      cq_ref, ckv_ref, ckvt_ref, kidxt_ref, widx_ref,
                      *, q_lora, kv_lora, idx_dim, w_scale):
    hn = _rms(h_ref[...], g_ref[...]).astype(BF16)
    proj = _dot(hn, w_ref[...])
    cq_ref[...] = _rms(proj[:, :q_lora], qg_ref[...]).astype(BF16)
    ckv = _rms(proj[:, q_lora:q_lora + kv_lora], kvg_ref[...])
    ckv_ref[...] = ckv.astype(BF16)
    ckvt_ref[...] = ckv.T.astype(BF16)

    o = q_lora + kv_lora
    kx = proj[:, o:o + LANES]
    valid = lax.broadcasted_iota(jnp.int32, (1, LANES), 1) < idx_dim
    mu = jnp.sum(kx, axis=-1, keepdims=True) * (1.0 / idx_dim)
    dlt = jnp.where(valid, kx - mu, 0.0)
    var = jnp.sum(dlt * dlt, axis=-1, keepdims=True) * (1.0 / idx_dim)
    kn = dlt * lax.rsqrt(var + EPS) * lng_ref[...] + lnb_ref[...]
    kidxt_ref[...] = kn.T.astype(BF16)
    widx_ref[...] = proj[:, o + LANES:o + 2 * LANES] * w_scale


def _attn_proj(h, g, w, qg, kvg, lng, lnb, *, q_lora, kv_lora, idx_dim, w_scale, tm):
    n, d = h.shape
    cols = w.shape[1]
    row = lambda c: pl.BlockSpec((1, c), lambda i: (0, 0))
    return pl.pallas_call(
        functools.partial(_attn_proj_kernel, q_lora=q_lora, kv_lora=kv_lora,
                          idx_dim=idx_dim, w_scale=w_scale),
        grid=(n // tm,),
        in_specs=[
            pl.BlockSpec((tm, d), lambda i: (i, 0)),
            row(d),
            pl.BlockSpec((d, cols), lambda i: (0, 0)),
            row(q_lora), row(kv_lora), row(LANES), row(LANES),
        ],
        out_specs=[
            pl.BlockSpec((tm, q_lora), lambda i: (i, 0)),
            pl.BlockSpec((tm, kv_lora), lambda i: (i, 0)),
            pl.BlockSpec((kv_lora, tm), lambda i: (0, i)),
            pl.BlockSpec((LANES, tm), lambda i: (0, i)),
            pl.BlockSpec((tm, LANES), lambda i: (i, 0)),
        ],
        out_shape=[
            jax.ShapeDtypeStruct((n, q_lora), BF16),
            jax.ShapeDtypeStruct((n, kv_lora), BF16),
            jax.ShapeDtypeStruct((kv_lora, n), BF16),
            jax.ShapeDtypeStruct((LANES, n), BF16),
            jax.ShapeDtypeStruct((n, LANES), F32),
        ],
        compiler_params=_params(1),
        name="attn_proj",
    )(h, g, w, qg, kvg, lng, lnb)


def _attn_tile(c, kidxt_ref, ckvt_ref, ckv_ref,
               ql_s, qi_s, w_s, key_s, bias_s, ol_s, lg_s, m_s, d_s, e_s,
               *, n_heads, topk):
    tq = key_s.shape[0]
    s = (c + 1) * tq
    row = c * tq + lax.broadcasted_iota(jnp.int32, (tq, 1), 0)
    col = lax.broadcasted_iota(jnp.int32, (1, s), 1)
    causal = col <= row

    def count(pred):
        return jnp.sum(pred.astype(F32), axis=1, keepdims=True)

    if s <= topk:
        bias_s[:, :s] = jnp.where(causal, 0.0, -jnp.inf)
    else:
        def head_score(h):
            return w_s[h][:, 0:1] * jnp.maximum(_dot(qi_s[h], kidxt_ref[:, :s]), 0.0)

        def score_body(g, carry):
            h = g * SCORE_GROUP
            part = head_score(h)
            for d in range(1, SCORE_GROUP):
                part = part + head_score(h + d)
            bias_s[:, :s] += part
            return carry
        bias_s[:, :s] = jnp.zeros((tq, s), F32)
        lax.fori_loop(0, n_heads // SCORE_GROUP, score_body, 0)

        bits = pltpu.bitcast(bias_s[:, :s], jnp.int32)
        key = bits ^ ((bits >> 31) & 0x7FFFFFFF)
        key_s[:, :s] = jnp.where(causal, key, INT_MIN)

        def select_body(i, thr_u):
            cand_u = thr_u | jnp.left_shift(jnp.int32(1), 31 - i)
            ok = count(key_s[:, :s] >= (cand_u ^ INT_MIN)) >= topk
            return jnp.where(ok, cand_u, thr_u)
        thr_u = lax.fori_loop(0, 32, select_body, jnp.zeros((tq, 1), jnp.int32))
        thr = thr_u ^ INT_MIN

        key = key_s[:, :s]
        ge = (key >= thr) & causal
        bias_s[:, :s] = jnp.where(ge, 0.0, -jnp.inf)

        @pl.when(jnp.max(count(ge)) > topk)
        def _():
            gt = (key > thr) & causal
            eq = (key == thr) & causal
            keep = topk - count(gt)
            nbits = s.bit_length()

            def cut_body(i, cut):
                cand = cut | jnp.left_shift(jnp.int32(1), nbits - 1 - i)
                ok = count(eq & (col < cand)) <= keep
                return jnp.where(ok, cand, cut)
            cut = lax.fori_loop(0, nbits, cut_body, jnp.zeros((tq, 1), jnp.int32))
            bias_s[:, :s] = jnp.where(gt | (eq & (col < cut)), 0.0, -jnp.inf)

    def logits(q, b):
        lg = _dot(q, ckvt_ref[:, :s]) + bias_s[:, :s]
        lg_s[b, :, :s] = lg
        m_s[b] = jnp.broadcast_to(jnp.max(lg, axis=1, keepdims=True), (tq, LANES))

    def softmax(b):
        e = jnp.exp2(lg_s[b, :, :s] - m_s[b][:, 0:1])
        d_s[b] = jnp.broadcast_to(jnp.sum(e, axis=1, keepdims=True), (tq, LANES))
        e_s[b, :, :s] = e.astype(BF16)

    def pv(b):
        return (_dot(e_s[b, :, :s], ckv_ref[:s, :]) / d_s[b][:, 0:1]).astype(BF16)

    logits(ql_s[0], 0)
    softmax(0)
    logits(ql_s[1], 1)

    def pipe_body(p, carry):
        h = 2 * p + 1
        qa, qb = ql_s[h + 1], ql_s[h + 2]
        oa = pv(0)
        softmax(1)
        logits(qa, 0)
        ob = pv(1)
        softmax(0)
        logits(qb, 1)
        ol_s[h - 1] = oa
        ol_s[h] = ob
        return carry
    lax.fori_loop(0, n_heads // 2 - 1, pipe_body, 0)

    oa = pv(0)
    softmax(1)
    ol_s[n_heads - 2] = oa
    ol_s[n_heads - 1] = pv(1)


def _attn_kernel(cq_ref, widx_ref, kidxt_ref, ckvt_ref, ckv_ref,
                 wuq_ref, wuk_ref, wqi_ref, wuv_ref, o_ref,
                 ql_s, qi_s, w_s, key_s, bias_s, ol_s, lg_s, m_s, d_s, e_s,
                 *, n_heads, topk, qscale):
    tq, seq = key_s.shape
    hd = wuk_ref.shape[1]
    vd = wuv_ref.shape[2]
    cq = cq_ref[...]
    q_all = _dot(cq, wuq_ref[...]).astype(BF16)
    qi_all = _dot(cq, wqi_ref[...]).astype(BF16)
    widx = widx_ref[...]
    for h in range(n_heads):
        q_lat = _dot(q_all[:, h * hd:(h + 1) * hd], wuk_ref[h]) * qscale
        ql_s[h] = q_lat.astype(BF16)
        qi_s[h] = qi_all[:, h * LANES:(h + 1) * LANES]
        w_s[h] = jnp.broadcast_to(widx[:, h:h + 1], (tq, LANES))

    tile = functools.partial(
        _attn_tile, kidxt_ref=kidxt_ref, ckvt_ref=ckvt_ref, ckv_ref=ckv_ref,
        ql_s=ql_s, qi_s=qi_s, w_s=w_s, key_s=key_s, bias_s=bias_s, ol_s=ol_s,
        lg_s=lg_s, m_s=m_s, d_s=d_s, e_s=e_s, n_heads=n_heads, topk=topk)
    for c in range(seq // tq):
        pl.when(pl.program_id(1) == c)(functools.partial(tile, c))

    for h in range(n_heads):
        o_ref[:, h * vd:(h + 1) * vd] = _dot(ol_s[h], wuv_ref[h]).astype(BF16)


def _attn(cq, widx, kidxt, ckvt, ckv, wuq, wuk, wqi, wuv, *, batch, seq, topk, scale, tq):
    n, q_lora = cq.shape
    n_heads, hd, kv_lora = wuk.shape
    vd = wuv.shape[2]
    nq = seq // tq
    const2 = lambda a: pl.BlockSpec(a.shape, lambda b, q: (0, 0))
    const3 = lambda a: pl.BlockSpec(a.shape, lambda b, q: (0, 0, 0))
    return pl.pallas_call(
        functools.partial(_attn_kernel, n_heads=n_heads, topk=topk,
                          qscale=scale * LOG2_E),
        grid=(batch, nq),
        in_specs=[
            pl.BlockSpec((tq, q_lora), lambda b, q: (b * nq + q, 0)),
            pl.BlockSpec((tq, LANES), lambda b, q: (b * nq + q, 0)),
            pl.BlockSpec((LANES, seq), lambda b, q: (0, b)),
            pl.BlockSpec((kv_lora, seq), lambda b, q: (0, b)),
            pl.BlockSpec((seq, kv_lora), lambda b, q: (b, 0)),
            const2(wuq), const3(wuk), const2(wqi), const3(wuv),
        ],
        out_specs=pl.BlockSpec((tq, n_heads * vd), lambda b, q: (b * nq + q, 0)),
        out_shape=jax.ShapeDtypeStruct((n, n_heads * vd), BF16),
        scratch_shapes=[
            pltpu.VMEM((n_heads, tq, kv_lora), BF16),
            pltpu.VMEM((n_heads, tq, LANES), BF16),
            pltpu.VMEM((n_heads, tq, LANES), F32),
            pltpu.VMEM((tq, seq), jnp.int32),
            pltpu.VMEM((tq, seq), F32),
            pltpu.VMEM((n_heads, tq, kv_lora), BF16),
            pltpu.VMEM((2, tq, seq), F32),
            pltpu.VMEM((2, tq, LANES), F32),
            pltpu.VMEM((2, tq, LANES), F32),
            pltpu.VMEM((2, tq, seq), BF16),
        ],
        compiler_params=_params(2),
        name="dsa_attention",
    )(cq, widx, kidxt, ckvt, ckv, wuq, wuk, wqi, wuv)


def _out_proj_kernel(h_ref, o_ref, w_ref, out_ref):
    out_ref[...] = h_ref[...] + _dot(o_ref[...], w_ref[...])


def _out_proj(h, o, w, *, tm):
    n, d = h.shape
    k = o.shape[1]
    return pl.pallas_call(
        _out_proj_kernel,
        grid=(n // tm,),
        in_specs=[
            pl.BlockSpec((tm, d), lambda i: (i, 0)),
            pl.BlockSpec((tm, k), lambda i: (i, 0)),
            pl.BlockSpec((k, d), lambda i: (0, 0)),
        ],
        out_specs=pl.BlockSpec((tm, d), lambda i: (i, 0)),
        out_shape=jax.ShapeDtypeStruct((n, d), F32),
        compiler_params=_params(1),
        name="attn_out_proj",
    )(h, o, w)


def _pad_last(a, width):
    return jnp.pad(a, [(0, 0)] * (a.ndim - 1) + [(0, width - a.shape[-1])])


def _tile(n, want):
    t = min(n, want)
    assert n % t == 0, (n, t)
    return t


def kernel(x, norm_mix, norm_mlp, mlp_w1, mlp_w2, conv_in, conv_w, conv_out, attn_in,
           q_norm, kv_norm, w_uq, w_uk, w_uv, w_qidx, kidx_ln_g, kidx_ln_b, attn_out,
           final_norm):
    batch, seq, d = x.shape
    depth = norm_mix.shape[0]
    q_lora, n_heads, hd = w_uq.shape[1:]
    kv_lora = w_uk.shape[3]
    idx_heads, idx_dim = w_qidx.shape[2:]
    assert idx_heads == n_heads and idx_dim <= LANES and n_heads <= LANES
    assert n_heads % SCORE_GROUP == 0 and n_heads % 2 == 0 and n_heads >= 4
    topk = min(TOPK_MAX, seq // 4)
    n = batch * seq

    tm = _tile(seq, 512)
    tf = _tile(mlp_w1.shape[2], 1024)
    tc = _tile(d, 512)
    tq = _tile(seq, 256)

    h = x.reshape(n, d)
    row = lambda v: v.reshape(1, -1)
    gf = row(final_norm)
    for i in range(depth):
        j = i // 2
        g = row(norm_mix[i])
        if i % 2 == 0:
            cw = jnp.pad(conv_w[j], ((0, 8 - conv_w.shape[1]), (0, 0)))
            h = _conv(h, g, conv_in[j].astype(BF16), cw, conv_out[j].astype(BF16),
                      seq=seq, tm=tm, tc=tc)
        else:
            o = q_lora + kv_lora
            w = jnp.concatenate([
                attn_in[j][:, :o],
                _pad_last(attn_in[j][:, o:o + idx_dim], LANES),
                _pad_last(attn_in[j][:, o + idx_dim:], LANES)], axis=1).astype(BF16)
            cq, ckv, ckvt, kidxt, widx = _attn_proj(
                h, g, w, row(q_norm[j]), row(kv_norm[j]),
                _pad_last(row(kidx_ln_g[j]), LANES), _pad_last(row(kidx_ln_b[j]), LANES),
                q_lora=q_lora, kv_lora=kv_lora, idx_dim=idx_dim,
                w_scale=float(idx_heads ** -0.5 * idx_dim ** -0.5), tm=tm)
            wqi = _pad_last(w_qidx[j], LANES).reshape(q_lora, n_heads * LANES)
            oh = _attn(cq, widx, kidxt, ckvt, ckv,
                       w_uq[j].reshape(q_lora, n_heads * hd).astype(BF16),
                       w_uk[j].astype(BF16), wqi.astype(BF16), w_uv[j].astype(BF16),
                       batch=batch, seq=seq, topk=topk, scale=float(hd ** -0.5), tq=tq)
            h = _out_proj(h, oh, attn_out[j].astype(BF16), tm=tm)
        h = _mlp(h, row(norm_mlp[i]), mlp_w1[i].astype(BF16), mlp_w2[i].astype(BF16), gf,
                 final=(i == depth - 1), tm=tm, tf=tf)
    return h.reshape(batch, seq, d)
```

```python
import functools

import jax
import jax.numpy as jnp
from jax import lax
from jax.experimental import pallas as pl
from jax.experimental.pallas import tpu as pltpu

EPS = 1e-6
TOPK_MAX = 256
LANES = 128
SUBLANES = 8
VMEM_LIMIT_BYTES = 56 * 1024 * 1024
INT_MIN = -2 ** 31
LOG2_E = 1.4426950408889634
SCORE_GROUP = 4
REDUCE_SLAB = 64

F32 = jnp.float32
BF16 = jnp.bfloat16


def _params(n_axes):
    return pltpu.CompilerParams(
        dimension_semantics=("arbitrary",) * n_axes,
        vmem_limit_bytes=VMEM_LIMIT_BYTES)


def _rms(x, g):
    return x * lax.rsqrt(jnp.mean(x * x, axis=-1, keepdims=True) + EPS) * g


def _dot(a, b):
    return jnp.dot(a, b, preferred_element_type=F32)


def _reduce_rows(op, x):
    rows, cols = x.shape
    if rows % REDUCE_SLAB == 0 and rows > REDUCE_SLAB:
        x = op(x.reshape(rows // REDUCE_SLAB, REDUCE_SLAB, cols), axis=0)
    return op(x, axis=0, keepdims=True)


def _mlp_kernel(h_ref, g_ref, w1_ref, w2_ref, gf_ref, o_ref, hn_ref, *, final):
    j = pl.program_id(1)

    @pl.when(j == 0)
    def _():
        h = h_ref[...]
        hn_ref[...] = _rms(h, g_ref[...]).astype(BF16)
        o_ref[...] = h

    a = jnp.maximum(_dot(hn_ref[...], w1_ref[...]), 0.0)
    o_ref[...] += _dot((a * a).astype(BF16), w2_ref[...])

    if final:
        @pl.when(j == pl.num_programs(1) - 1)
        def _():
            o_ref[...] = _rms(o_ref[...], gf_ref[...])


def _mlp(h, g, w1, w2, gf, *, final, tm, tf):
    n, d = h.shape
    ff = w1.shape[1]
    return pl.pallas_call(
        functools.partial(_mlp_kernel, final=final),
        grid=(n // tm, ff // tf),
        in_specs=[
            pl.BlockSpec((tm, d), lambda i, j: (i, 0)),
            pl.BlockSpec((1, d), lambda i, j: (0, 0)),
            pl.BlockSpec((d, tf), lambda i, j: (0, j)),
            pl.BlockSpec((tf, d), lambda i, j: (j, 0)),
            pl.BlockSpec((1, d), lambda i, j: (0, 0)),
        ],
        out_specs=pl.BlockSpec((tm, d), lambda i, j: (i, 0)),
        out_shape=jax.ShapeDtypeStruct((n, d), F32),
        scratch_shapes=[pltpu.VMEM((tm, d), BF16)],
        compiler_params=_params(2),
        name="sqrelu_mlp",
    )(h, g, w1, w2, gf)


def _conv_kernel(h_ref, g_ref, wb_ref, wc_ref, wx_ref, cw_ref, wo_ref, o_ref,
                 hn_ref, halo_ref, *, tiles_per_seq):
    i = pl.program_id(0)
    j = pl.program_id(1)

    @pl.when(j == 0)
    def _():
        h = h_ref[...]
        hn_ref[...] = _rms(h, g_ref[...]).astype(BF16)
        o_ref[...] = h

    hn = hn_ref[...]
    bg = _dot(hn, wb_ref[...])
    z = _dot(hn, wc_ref[...]) * _dot(hn, wx_ref[...])
    tm = z.shape[0]

    @pl.when(i % tiles_per_seq == 0)
    def _():
        halo_ref[j] = jnp.zeros(halo_ref.shape[1:], F32)

    prev = halo_ref[j]
    halo_ref[j] = z[tm - 8:, :]
    p1 = prev[7:8, :]
    p2 = prev[6:7, :]

    row = lax.broadcasted_iota(jnp.int32, (tm, 1), 0)
    z1 = jnp.where(row == 0, p1, pltpu.roll(z, 1, 0))
    z2 = jnp.where(row == 0, p2, jnp.where(row == 1, p1, pltpu.roll(z, 2, 0)))
    cw = cw_ref[...]
    zc = cw[0:1, :] * z2 + cw[1:2, :] * z1 + cw[2:3, :] * z
    o_ref[...] += _dot((bg * zc).astype(BF16), wo_ref[...])


def _conv(h, g, w_in, cw, w_out, *, seq, tm, tc):
    n, d = h.shape
    nc = d // tc
    return pl.pallas_call(
        functools.partial(_conv_kernel, tiles_per_seq=seq // tm),
        grid=(n // tm, nc),
        in_specs=[
            pl.BlockSpec((tm, d), lambda i, j: (i, 0)),
            pl.BlockSpec((1, d), lambda i, j: (0, 0)),
            pl.BlockSpec((d, tc), lambda i, j: (0, j)),
            pl.BlockSpec((d, tc), lambda i, j: (0, j + nc)),
            pl.BlockSpec((d, tc), lambda i, j: (0, j + 2 * nc)),
            pl.BlockSpec((8, tc), lambda i, j: (0, j)),
            pl.BlockSpec((tc, d), lambda i, j: (j, 0)),
        ],
        out_specs=pl.BlockSpec((tm, d), lambda i, j: (i, 0)),
        out_shape=jax.ShapeDtypeStruct((n, d), F32),
        scratch_shapes=[pltpu.VMEM((tm, d), BF16), pltpu.VMEM((nc, 8, tc), F32)],
        compiler_params=_params(2),
        name="short_conv",
    )(h, g, w_in, w_in, w_in, cw, w_out)


def _attn_proj_kernel(h_ref, g_ref, w_ref, qg_ref, kvg_ref, lng_ref, lnb_ref,
                      cqt_ref, ckv_ref, ckvt_ref, kidx_ref, widxt_ref,
                      *, q_lora, kv_lora, idx_dim, w_scale):
    hn = _rms(h_ref[...], g_ref[...]).astype(BF16)
    proj = _dot(hn, w_ref[...])
    cqt_ref[...] = _rms(proj[:, :q_lora], qg_ref[...]).T.astype(BF16)
    ckv = _rms(proj[:, q_lora:q_lora + kv_lora], kvg_ref[...])
    ckv_ref[...] = ckv.astype(BF16)
    ckvt_ref[...] = ckv.T.astype(BF16)

    o = q_lora + kv_lora
    kx = proj[:, o:o + LANES]
    valid = lax.broadcasted_iota(jnp.int32, (1, LANES), 1) < idx_dim
    mu = jnp.sum(kx, axis=-1, keepdims=True) * (1.0 / idx_dim)
    dlt = jnp.where(valid, kx - mu, 0.0)
    var = jnp.sum(dlt * dlt, axis=-1, keepdims=True) * (1.0 / idx_dim)
    kidx_ref[...] = (dlt * lax.rsqrt(var + EPS) * lng_ref[...] + lnb_ref[...]).astype(BF16)
    widxt_ref[...] = (proj[:, o + LANES:o + 2 * LANES] * w_scale).T


def _attn_proj(h, g, w, qg, kvg, lng, lnb, *, q_lora, kv_lora, idx_dim, w_scale, tm):
    n, d = h.shape
    cols = w.shape[1]
    row = lambda c: pl.BlockSpec((1, c), lambda i: (0, 0))
    return pl.pallas_call(
        functools.partial(_attn_proj_kernel, q_lora=q_lora, kv_lora=kv_lora,
                          idx_dim=idx_dim, w_scale=w_scale),
        grid=(n // tm,),
        in_specs=[
            pl.BlockSpec((tm, d), lambda i: (i, 0)),
            row(d),
            pl.BlockSpec((d, cols), lambda i: (0, 0)),
            row(q_lora), row(kv_lora), row(LANES), row(LANES),
        ],
        out_specs=[
            pl.BlockSpec((q_lora, tm), lambda i: (0, i)),
            pl.BlockSpec((tm, kv_lora), lambda i: (i, 0)),
            pl.BlockSpec((kv_lora, tm), lambda i: (0, i)),
            pl.BlockSpec((tm, LANES), lambda i: (i, 0)),
            pl.BlockSpec((LANES, tm), lambda i: (0, i)),
        ],
        out_shape=[
            jax.ShapeDtypeStruct((q_lora, n), BF16),
            jax.ShapeDtypeStruct((n, kv_lora), BF16),
            jax.ShapeDtypeStruct((kv_lora, n), BF16),
            jax.ShapeDtypeStruct((n, LANES), BF16),
            jax.ShapeDtypeStruct((LANES, n), F32),
        ],
        compiler_params=_params(1),
        name="attn_proj",
    )(h, g, w, qg, kvg, lng, lnb)


def _attn_tile(c, kidx_ref, ckv_ref, ckvt_ref,
               ql_s, qi_s, w_s, key_s, bias_s, ol_s, lg_s, m_s, d_s, e_s,
               *, n_heads, topk):
    tq = key_s.shape[1]
    s = (c + 1) * tq
    kpos = lax.broadcasted_iota(jnp.int32, (s, 1), 0)
    qpos = c * tq + lax.broadcasted_iota(jnp.int32, (1, tq), 1)
    causal = kpos <= qpos

    def count(pred):
        return _reduce_rows(jnp.sum, pred.astype(F32))

    if s <= topk:
        bias_s[:s, :] = jnp.where(causal, 0.0, -jnp.inf)
    else:
        def head_score(h):
            return w_s[h][0:1, :] * jnp.maximum(_dot(kidx_ref[:s, :], qi_s[h]), 0.0)

        def score_body(g, carry):
            h = g * SCORE_GROUP
            part = head_score(h)
            for d in range(1, SCORE_GROUP):
                part = part + head_score(h + d)
            bias_s[:s, :] += part
            return carry
        bias_s[:s, :] = jnp.zeros((s, tq), F32)
        lax.fori_loop(0, n_heads // SCORE_GROUP, score_body, 0)

        bits = pltpu.bitcast(bias_s[:s, :], jnp.int32)
        key = bits ^ ((bits >> 31) & 0x7FFFFFFF)
        key_s[:s, :] = jnp.where(causal, key, INT_MIN)

        def select_body(i, thr_u):
            cand_u = thr_u | jnp.left_shift(jnp.int32(1), 31 - i)
            ok = count(key_s[:s, :] >= (cand_u ^ INT_MIN)) >= topk
            return jnp.where(ok, cand_u, thr_u)
        thr_u = lax.fori_loop(0, 32, select_body, jnp.zeros((1, tq), jnp.int32))
        thr = thr_u ^ INT_MIN

        key = key_s[:s, :]
        ge = key >= thr
        if c * tq < topk:
            ge = ge & causal
        bias_s[:s, :] = jnp.where(ge, 0.0, -jnp.inf)

        @pl.when(jnp.max(count(ge)) > topk)
        def _():
            gt = (key > thr) & causal
            eq = (key == thr) & causal
            keep = topk - count(gt)
            nbits = s.bit_length()

            def cut_body(i, cut):
                cand = cut | jnp.left_shift(jnp.int32(1), nbits - 1 - i)
                ok = count(eq & (kpos < cand)) <= keep
                return jnp.where(ok, cand, cut)
            cut = lax.fori_loop(0, nbits, cut_body, jnp.zeros((1, tq), jnp.int32))
            bias_s[:s, :] = jnp.where(gt | (eq & (kpos < cut)), 0.0, -jnp.inf)

    def logits(q, b):
        lg = _dot(ckv_ref[:s, :], q) + bias_s[:s, :]
        lg_s[b, :s, :] = lg
        m_s[b] = jnp.broadcast_to(_reduce_rows(jnp.max, lg), (SUBLANES, tq))

    def softmax(b):
        e = jnp.exp2(lg_s[b, :s, :] - m_s[b][0:1, :])
        d_s[b] = jnp.broadcast_to(_reduce_rows(jnp.sum, e), (SUBLANES, tq))
        e_s[b, :s, :] = e.astype(BF16)

    def pv(b):
        return (_dot(ckvt_ref[:, :s], e_s[b, :s, :]) / d_s[b][0:1, :]).astype(BF16)

    logits(ql_s[0], 0)
    softmax(0)
    logits(ql_s[1], 1)

    def pipe_body(p, carry):
        h = 2 * p + 1
        qa, qb = ql_s[h + 1], ql_s[h + 2]
        oa = pv(0)
        softmax(1)
        logits(qa, 0)
        ob = pv(1)
        softmax(0)
        logits(qb, 1)
        ol_s[h - 1] = oa
        ol_s[h] = ob
        return carry
    lax.fori_loop(0, n_heads // 2 - 1, pipe_body, 0)

    oa = pv(0)
    softmax(1)
    ol_s[n_heads - 2] = oa
    ol_s[n_heads - 1] = pv(1)


def _attn_kernel(cqt_ref, widxt_ref, kidx_ref, ckv_ref, ckvt_ref,
                 wuqt_ref, wukt_ref, wqit_ref, wuvt_ref, o_ref,
                 ql_s, qi_s, w_s, key_s, bias_s, ol_s, lg_s, m_s, d_s, e_s,
                 *, n_heads, topk, qscale):
    seq, tq = key_s.shape
    hd = wukt_ref.shape[2]
    vd = wuvt_ref.shape[1]
    cqt = cqt_ref[...]
    q_all = _dot(wuqt_ref[...], cqt).astype(BF16)
    qi_all = _dot(wqit_ref[...], cqt).astype(BF16)
    widxt = widxt_ref[...]
    for h in range(n_heads):
        q_lat = _dot(wukt_ref[h], q_all[h * hd:(h + 1) * hd, :]) * qscale
        ql_s[h] = q_lat.astype(BF16)
        qi_s[h] = qi_all[h * LANES:(h + 1) * LANES, :]
        w_s[h] = jnp.broadcast_to(widxt[h:h + 1, :], (SUBLANES, tq))

    tile = functools.partial(
        _attn_tile, kidx_ref=kidx_ref, ckv_ref=ckv_ref, ckvt_ref=ckvt_ref,
        ql_s=ql_s, qi_s=qi_s, w_s=w_s, key_s=key_s, bias_s=bias_s, ol_s=ol_s,
        lg_s=lg_s, m_s=m_s, d_s=d_s, e_s=e_s, n_heads=n_heads, topk=topk)
    for c in range(seq // tq):
        pl.when(pl.program_id(1) == c)(functools.partial(tile, c))

    for h in range(n_heads):
        o_ref[:, h * vd:(h + 1) * vd] = _dot(wuvt_ref[h], ol_s[h]).T.astype(BF16)


def _attn(cqt, widxt, kidx, ckv, ckvt, wuqt, wukt, wqit, wuvt, *, batch, seq, topk, scale, tq):
    q_lora, n = cqt.shape
    n_heads, kv_lora, hd = wukt.shape
    vd = wuvt.shape[1]
    nq = seq // tq
    const2 = lambda a: pl.BlockSpec(a.shape, lambda b, q: (0, 0))
    const3 = lambda a: pl.BlockSpec(a.shape, lambda b, q: (0, 0, 0))
    return pl.pallas_call(
        functools.partial(_attn_kernel, n_heads=n_heads, topk=topk,
                          qscale=scale * LOG2_E),
        grid=(batch, nq),
        in_specs=[
            pl.BlockSpec((q_lora, tq), lambda b, q: (0, b * nq + q)),
            pl.BlockSpec((LANES, tq), lambda b, q: (0, b * nq + q)),
            pl.BlockSpec((seq, LANES), lambda b, q: (b, 0)),
            pl.BlockSpec((seq, kv_lora), lambda b, q: (b, 0)),
            pl.BlockSpec((kv_lora, seq), lambda b, q: (0, b)),
            const2(wuqt), const3(wukt), const2(wqit), const3(wuvt),
        ],
        out_specs=pl.BlockSpec((tq, n_heads * vd), lambda b, q: (b * nq + q, 0)),
        out_shape=jax.ShapeDtypeStruct((n, n_heads * vd), BF16),
        scratch_shapes=[
            pltpu.VMEM((n_heads, kv_lora, tq), BF16),
            pltpu.VMEM((n_heads, LANES, tq), BF16),
            pltpu.VMEM((n_heads, SUBLANES, tq), F32),
            pltpu.VMEM((seq, tq), jnp.int32),
            pltpu.VMEM((seq, tq), F32),
            pltpu.VMEM((n_heads, kv_lora, tq), BF16),
            pltpu.VMEM((2, seq, tq), F32),
            pltpu.VMEM((2, SUBLANES, tq), F32),
            pltpu.VMEM((2, SUBLANES, tq), F32),
            pltpu.VMEM((2, seq, tq), BF16),
        ],
        compiler_params=_params(2),
        name="dsa_attention",
    )(cqt, widxt, kidx, ckv, ckvt, wuqt, wukt, wqit, wuvt)


def _out_proj_kernel(h_ref, o_ref, w_ref, out_ref):
    out_ref[...] = h_ref[...] + _dot(o_ref[...], w_ref[...])


def _out_proj(h, o, w, *, tm):
    n, d = h.shape
    k = o.shape[1]
    return pl.pallas_call(
        _out_proj_kernel,
        grid=(n // tm,),
        in_specs=[
            pl.BlockSpec((tm, d), lambda i: (i, 0)),
            pl.BlockSpec((tm, k), lambda i: (i, 0)),
            pl.BlockSpec((k, d), lambda i: (0, 0)),
        ],
        out_specs=pl.BlockSpec((tm, d), lambda i: (i, 0)),
        out_shape=jax.ShapeDtypeStruct((n, d), F32),
        compiler_params=_params(1),
        name="attn_out_proj",
    )(h, o, w)


def _pad_last(a, width):
    return jnp.pad(a, [(0, 0)] * (a.ndim - 1) + [(0, width - a.shape[-1])])


def _tile(n, want):
    t = min(n, want)
    assert n % t == 0, (n, t)
    return t


def kernel(x, norm_mix, norm_mlp, mlp_w1, mlp_w2, conv_in, conv_w, conv_out, attn_in,
           q_norm, kv_norm, w_uq, w_uk, w_uv, w_qidx, kidx_ln_g, kidx_ln_b, attn_out,
           final_norm):
    batch, seq, d = x.shape
    depth = norm_mix.shape[0]
    q_lora, n_heads, hd = w_uq.shape[1:]
    kv_lora = w_uk.shape[3]
    idx_heads, idx_dim = w_qidx.shape[2:]
    assert idx_heads == n_heads and idx_dim <= LANES and n_heads <= LANES
    assert n_heads % SCORE_GROUP == 0 and n_heads % 2 == 0 and n_heads >= 4
    topk = min(TOPK_MAX, seq // 4)
    n = batch * seq

    tm = _tile(seq, 512)
    tf = _tile(mlp_w1.shape[2], 1024)
    tc = _tile(d, 512)
    tq = _tile(seq, 256)

    h = x.reshape(n, d)
    row = lambda v: v.reshape(1, -1)
    gf = row(final_norm)
    for i in range(depth):
        j = i // 2
        g = row(norm_mix[i])
        if i % 2 == 0:
            cw = jnp.pad(conv_w[j], ((0, 8 - conv_w.shape[1]), (0, 0)))
            h = _conv(h, g, conv_in[j].astype(BF16), cw, conv_out[j].astype(BF16),
                      seq=seq, tm=tm, tc=tc)
        else:
            o = q_lora + kv_lora
            w = jnp.concatenate([
                attn_in[j][:, :o],
                _pad_last(attn_in[j][:, o:o + idx_dim], LANES),
                _pad_last(attn_in[j][:, o + idx_dim:], LANES)], axis=1).astype(BF16)
            cqt, ckv, ckvt, kidx, widxt = _attn_proj(
                h, g, w, row(q_norm[j]), row(kv_norm[j]),
                _pad_last(row(kidx_ln_g[j]), LANES), _pad_last(row(kidx_ln_b[j]), LANES),
                q_lora=q_lora, kv_lora=kv_lora, idx_dim=idx_dim,
                w_scale=float(idx_heads ** -0.5 * idx_dim ** -0.5), tm=tm)
            wuqt = w_uq[j].reshape(q_lora, n_heads * hd).T.astype(BF16)
            wukt = w_uk[j].transpose(0, 2, 1).astype(BF16)
            wqit = _pad_last(w_qidx[j], LANES).reshape(q_lora, n_heads * LANES).T.astype(BF16)
            wuvt = w_uv[j].transpose(0, 2, 1).astype(BF16)
            oh = _attn(cqt, widxt, kidx, ckv, ckvt, wuqt, wukt, wqit, wuvt,
                       batch=batch, seq=seq, topk=topk, scale=float(hd ** -0.5), tq=tq)
            h = _out_proj(h, oh, attn_out[j].astype(BF16), tm=tm)
        h = _mlp(h, row(norm_mlp[i]), mlp_w1[i].astype(BF16), mlp_w2[i].astype(BF16), gf,
                 final=(i == depth - 1), tm=tm, tf=tf)
    return h.reshape(batch, seq, d)
```

```python
import functools

import jax
import jax.numpy as jnp
from jax import lax
from jax.experimental import pallas as pl
from jax.experimental.pallas import tpu as pltpu

EPS = 1e-6
TOPK_MAX = 256
LANES = 128
SUBLANES = 8
VMEM_LIMIT_BYTES = 56 * 1024 * 1024
INT_MIN = -2 ** 31
LOG2_E = 1.4426950408889634
SCORE_GROUP = 16
REDUCE_SLAB = 64
PIPE_UNROLL = 2

F32 = jnp.float32
BF16 = jnp.bfloat16


def _params(n_axes):
    return pltpu.CompilerParams(
        dimension_semantics=("arbitrary",) * n_axes,
        vmem_limit_bytes=VMEM_LIMIT_BYTES)


def _rms(x, g):
    return x * lax.rsqrt(jnp.mean(x * x, axis=-1, keepdims=True) + EPS) * g


def _dot(a, b):
    return jnp.dot(a, b, preferred_element_type=F32)


def _reduce_rows(op, x):
    rows, cols = x.shape
    if rows % REDUCE_SLAB == 0 and rows > REDUCE_SLAB:
        x = op(x.reshape(rows // REDUCE_SLAB, REDUCE_SLAB, cols), axis=0)
    return op(x, axis=0, keepdims=True)


def _mlp_kernel(h_ref, g_ref, w1_ref, w2_ref, gf_ref, o_ref, hn_ref, *, final):
    j = pl.program_id(1)

    @pl.when(j == 0)
    def _():
        h = h_ref[...]
        hn_ref[...] = _rms(h, g_ref[...]).astype(BF16)
        o_ref[...] = h

    a = jnp.maximum(_dot(hn_ref[...], w1_ref[...]), 0.0)
    o_ref[...] += _dot((a * a).astype(BF16), w2_ref[...])

    if final:
        @pl.when(j == pl.num_programs(1) - 1)
        def _():
            o_ref[...] = _rms(o_ref[...], gf_ref[...])


def _mlp(h, g, w1, w2, gf, *, final, tm, tf):
    n, d = h.shape
    ff = w1.shape[1]
    return pl.pallas_call(
        functools.partial(_mlp_kernel, final=final),
        grid=(n // tm, ff // tf),
        in_specs=[
            pl.BlockSpec((tm, d), lambda i, j: (i, 0)),
            pl.BlockSpec((1, d), lambda i, j: (0, 0)),
            pl.BlockSpec((d, tf), lambda i, j: (0, j)),
            pl.BlockSpec((tf, d), lambda i, j: (j, 0)),
            pl.BlockSpec((1, d), lambda i, j: (0, 0)),
        ],
        out_specs=pl.BlockSpec((tm, d), lambda i, j: (i, 0)),
        out_shape=jax.ShapeDtypeStruct((n, d), F32),
        scratch_shapes=[pltpu.VMEM((tm, d), BF16)],
        compiler_params=_params(2),
        name="sqrelu_mlp",
    )(h, g, w1, w2, gf)


def _conv_kernel(h_ref, g_ref, wb_ref, wc_ref, wx_ref, cw_ref, wo_ref, o_ref,
                 hn_ref, halo_ref, *, tiles_per_seq):
    i = pl.program_id(0)
    j = pl.program_id(1)

    @pl.when(j == 0)
    def _():
        h = h_ref[...]
        hn_ref[...] = _rms(h, g_ref[...]).astype(BF16)
        o_ref[...] = h

    hn = hn_ref[...]
    bg = _dot(hn, wb_ref[...])
    z = _dot(hn, wc_ref[...]) * _dot(hn, wx_ref[...])
    tm = z.shape[0]

    @pl.when(i % tiles_per_seq == 0)
    def _():
        halo_ref[j] = jnp.zeros(halo_ref.shape[1:], F32)

    prev = halo_ref[j]
    halo_ref[j] = z[tm - 8:, :]
    p1 = prev[7:8, :]
    p2 = prev[6:7, :]

    row = lax.broadcasted_iota(jnp.int32, (tm, 1), 0)
    z1 = jnp.where(row == 0, p1, pltpu.roll(z, 1, 0))
    z2 = jnp.where(row == 0, p2, jnp.where(row == 1, p1, pltpu.roll(z, 2, 0)))
    cw = cw_ref[...]
    zc = cw[0:1, :] * z2 + cw[1:2, :] * z1 + cw[2:3, :] * z
    o_ref[...] += _dot((bg * zc).astype(BF16), wo_ref[...])


def _conv(h, g, w_in, cw, w_out, *, seq, tm, tc):
    n, d = h.shape
    nc = d // tc
    return pl.pallas_call(
        functools.partial(_conv_kernel, tiles_per_seq=seq // tm),
        grid=(n // tm, nc),
        in_specs=[
            pl.BlockSpec((tm, d), lambda i, j: (i, 0)),
            pl.BlockSpec((1, d), lambda i, j: (0, 0)),
            pl.BlockSpec((d, tc), lambda i, j: (0, j)),
            pl.BlockSpec((d, tc), lambda i, j: (0, j + nc)),
            pl.BlockSpec((d, tc), lambda i, j: (0, j + 2 * nc)),
            pl.BlockSpec((8, tc), lambda i, j: (0, j)),
            pl.BlockSpec((tc, d), lambda i, j: (j, 0)),
        ],
        out_specs=pl.BlockSpec((tm, d), lambda i, j: (i, 0)),
        out_shape=jax.ShapeDtypeStruct((n, d), F32),
        scratch_shapes=[pltpu.VMEM((tm, d), BF16), pltpu.VMEM((nc, 8, tc), F32)],
        compiler_params=_params(2),
        name="short_conv",
    )(h, g, w_in, w_in, w_in, cw, w_out)


def _attn_proj_kernel(h_ref, g_ref, w_ref, qg_ref, kvg_ref, lng_ref, lnb_ref,
                      cqt_ref, ckv_ref, ckvt_ref, kidx_ref, widxt_ref,
                      *, q_lora, kv_lora, idx_dim, w_scale):
    hn = _rms(h_ref[...], g_ref[...]).astype(BF16)
    proj = _dot(hn, w_ref[...])
    cqt_ref[...] = _rms(proj[:, :q_lora], qg_ref[...]).T.astype(BF16)
    ckv = _rms(proj[:, q_lora:q_lora + kv_lora], kvg_ref[...])
    ckv_ref[...] = ckv.astype(BF16)
    ckvt_ref[...] = ckv.T.astype(BF16)

    o = q_lora + kv_lora
    kx = proj[:, o:o + LANES]
    valid = lax.broadcasted_iota(jnp.int32, (1, LANES), 1) < idx_dim
    mu = jnp.sum(kx, axis=-1, keepdims=True) * (1.0 / idx_dim)
    dlt = jnp.where(valid, kx - mu, 0.0)
    var = jnp.sum(dlt * dlt, axis=-1, keepdims=True) * (1.0 / idx_dim)
    kidx_ref[...] = (dlt * lax.rsqrt(var + EPS) * lng_ref[...] + lnb_ref[...]).astype(BF16)
    widxt_ref[...] = (proj[:, o + LANES:o + 2 * LANES] * w_scale).T


def _attn_proj(h, g, w, qg, kvg, lng, lnb, *, q_lora, kv_lora, idx_dim, w_scale, tm):
    n, d = h.shape
    cols = w.shape[1]
    row = lambda c: pl.BlockSpec((1, c), lambda i: (0, 0))
    return pl.pallas_call(
        functools.partial(_attn_proj_kernel, q_lora=q_lora, kv_lora=kv_lora,
                          idx_dim=idx_dim, w_scale=w_scale),
        grid=(n // tm,),
        in_specs=[
            pl.BlockSpec((tm, d), lambda i: (i, 0)),
            row(d),
            pl.BlockSpec((d, cols), lambda i: (0, 0)),
            row(q_lora), row(kv_lora), row(LANES), row(LANES),
        ],
        out_specs=[
            pl.BlockSpec((q_lora, tm), lambda i: (0, i)),
            pl.BlockSpec((tm, kv_lora), lambda i: (i, 0)),
            pl.BlockSpec((kv_lora, tm), lambda i: (0, i)),
            pl.BlockSpec((tm, LANES), lambda i: (i, 0)),
            pl.BlockSpec((LANES, tm), lambda i: (0, i)),
        ],
        out_shape=[
            jax.ShapeDtypeStruct((q_lora, n), BF16),
            jax.ShapeDtypeStruct((n, kv_lora), BF16),
            jax.ShapeDtypeStruct((kv_lora, n), BF16),
            jax.ShapeDtypeStruct((n, LANES), BF16),
            jax.ShapeDtypeStruct((LANES, n), F32),
        ],
        compiler_params=_params(1),
        name="attn_proj",
    )(h, g, w, qg, kvg, lng, lnb)


def _attn_tile(c, kidx_ref, ckv_ref, ckvt_ref,
               ql_s, qi_s, w_s, key_s, bias_s, ol_s, lg_s, m_s, d_s, e_s,
               *, n_heads, topk):
    tq = key_s.shape[1]
    s = (c + 1) * tq
    kpos = lax.broadcasted_iota(jnp.int32, (s, 1), 0)
    qpos = c * tq + lax.broadcasted_iota(jnp.int32, (1, tq), 1)
    causal = kpos <= qpos

    def count(pred):
        return _reduce_rows(jnp.sum, pred.astype(F32))

    if s <= topk:
        bias_s[:s, :] = jnp.where(causal, 0.0, -jnp.inf)
    else:
        def head_score(h):
            return w_s[h][0:1, :] * jnp.maximum(_dot(kidx_ref[:s, :], qi_s[h]), 0.0)

        def group_score(h):
            part = head_score(h)
            for d in range(1, SCORE_GROUP):
                part = part + head_score(h + d)
            return part

        def score_body(g, carry):
            bias_s[:s, :] += group_score(g * SCORE_GROUP)
            return carry
        bias_s[:s, :] = group_score(0)
        if n_heads > SCORE_GROUP:
            lax.fori_loop(1, n_heads // SCORE_GROUP, score_body, 0)

        bits = pltpu.bitcast(bias_s[:s, :], jnp.int32)
        key = bits ^ ((bits >> 31) & 0x7FFFFFFF)
        key_s[:s, :] = jnp.where(causal, key, INT_MIN)

        def select_body(i, thr_u):
            cand_u = thr_u | jnp.left_shift(jnp.int32(1), 31 - i)
            ok = count(key_s[:s, :] >= (cand_u ^ INT_MIN)) >= topk
            return jnp.where(ok, cand_u, thr_u)
        thr_u = lax.fori_loop(0, 32, select_body, jnp.zeros((1, tq), jnp.int32))
        thr = thr_u ^ INT_MIN

        key = key_s[:s, :]
        ge = key >= thr
        if c * tq < topk:
            ge = ge & causal
        bias_s[:s, :] = jnp.where(ge, 0.0, -jnp.inf)

        @pl.when(jnp.max(count(ge)) > topk)
        def _():
            gt = (key > thr) & causal
            eq = (key == thr) & causal
            keep = topk - count(gt)
            nbits = s.bit_length()

            def cut_body(i, cut):
                cand = cut | jnp.left_shift(jnp.int32(1), nbits - 1 - i)
                ok = count(eq & (kpos < cand)) <= keep
                return jnp.where(ok, cand, cut)
            cut = lax.fori_loop(0, nbits, cut_body, jnp.zeros((1, tq), jnp.int32))
            bias_s[:s, :] = jnp.where(gt | (eq & (kpos < cut)), 0.0, -jnp.inf)

    def logits(q, b):
        lg = _dot(ckv_ref[:s, :], q) + bias_s[:s, :]
        lg_s[b, :s, :] = lg
        m_s[b] = jnp.broadcast_to(_reduce_rows(jnp.max, lg), (SUBLANES, tq))

    def softmax(b):
        e = jnp.exp2(lg_s[b, :s, :] - m_s[b][0:1, :])
        d_s[b] = jnp.broadcast_to(_reduce_rows(jnp.sum, e), (SUBLANES, tq))
        e_s[b, :s, :] = e.astype(BF16)

    def pv(b):
        return (_dot(ckvt_ref[:, :s], e_s[b, :s, :]) / d_s[b][0:1, :]).astype(BF16)

    def steps(t0, count):
        qs = [ql_s[t0 + u] for u in range(count)]
        outs = []
        for u in range(count):
            outs.append(pv(u % 2))
            softmax((u + 1) % 2)
            logits(qs[u], u % 2)
        for u in range(count):
            ol_s[t0 + u - 2] = outs[u]

    logits(ql_s[0], 0)
    softmax(0)
    logits(ql_s[1], 1)

    n_steady = n_heads - 2
    peeled = n_steady % PIPE_UNROLL
    if peeled:
        steps(2, peeled)
    if n_steady // PIPE_UNROLL > 1:
        def pipe_body(p, carry):
            steps(2 + peeled + PIPE_UNROLL * p, PIPE_UNROLL)
            return carry
        lax.fori_loop(0, n_steady // PIPE_UNROLL, pipe_body, 0)
    elif n_steady // PIPE_UNROLL == 1:
        steps(2 + peeled, PIPE_UNROLL)

    oa = pv(0)
    softmax(1)
    ol_s[n_heads - 2] = oa
    ol_s[n_heads - 1] = pv(1)


def _attn_kernel(cqt_ref, widxt_ref, kidx_ref, ckv_ref, ckvt_ref,
                 wuqt_ref, wukt_ref, wqit_ref, wuvt_ref, o_ref,
                 ql_s, qi_s, w_s, key_s, bias_s, ol_s, lg_s, m_s, d_s, e_s,
                 *, n_heads, topk, qscale):
    seq, tq = key_s.shape
    hd = wukt_ref.shape[2]
    vd = wuvt_ref.shape[1]
    cqt = cqt_ref[...]
    q_all = _dot(wuqt_ref[...], cqt).astype(BF16)
    qi_all = _dot(wqit_ref[...], cqt).astype(BF16)
    widxt = widxt_ref[...]
    for h in range(n_heads):
        q_lat = _dot(wukt_ref[h], q_all[h * hd:(h + 1) * hd, :]) * qscale
        ql_s[h] = q_lat.astype(BF16)
        qi_s[h] = qi_all[h * LANES:(h + 1) * LANES, :]
        w_s[h] = jnp.broadcast_to(widxt[h:h + 1, :], (SUBLANES, tq))

    tile = functools.partial(
        _attn_tile, kidx_ref=kidx_ref, ckv_ref=ckv_ref, ckvt_ref=ckvt_ref,
        ql_s=ql_s, qi_s=qi_s, w_s=w_s, key_s=key_s, bias_s=bias_s, ol_s=ol_s,
        lg_s=lg_s, m_s=m_s, d_s=d_s, e_s=e_s, n_heads=n_heads, topk=topk)
    for c in range(seq // tq):
        pl.when(pl.program_id(1) == c)(functools.partial(tile, c))

    for h in range(n_heads):
        o_ref[:, h * vd:(h + 1) * vd] = _dot(wuvt_ref[h], ol_s[h]).T.astype(BF16)


def _attn(cqt, widxt, kidx, ckv, ckvt, wuqt, wukt, wqit, wuvt, *, batch, seq, topk, scale, tq):
    q_lora, n = cqt.shape
    n_heads, kv_lora, hd = wukt.shape
    vd = wuvt.shape[1]
    nq = seq // tq
    const2 = lambda a: pl.BlockSpec(a.shape, lambda b, q: (0, 0))
    const3 = lambda a: pl.BlockSpec(a.shape, lambda b, q: (0, 0, 0))
    return pl.pallas_call(
        functools.partial(_attn_kernel, n_heads=n_heads, topk=topk,
                          qscale=scale * LOG2_E),
        grid=(batch, nq),
        in_specs=[
            pl.BlockSpec((q_lora, tq), lambda b, q: (0, b * nq + q)),
            pl.BlockSpec((LANES, tq), lambda b, q: (0, b * nq + q)),
            pl.BlockSpec((seq, LANES), lambda b, q: (b, 0)),
            pl.BlockSpec((seq, kv_lora), lambda b, q: (b, 0)),
            pl.BlockSpec((kv_lora, seq), lambda b, q: (0, b)),
            const2(wuqt), const3(wukt), const2(wqit), const3(wuvt),
        ],
        out_specs=pl.BlockSpec((tq, n_heads * vd), lambda b, q: (b * nq + q, 0)),
        out_shape=jax.ShapeDtypeStruct((n, n_heads * vd), BF16),
        scratch_shapes=[
            pltpu.VMEM((n_heads, kv_lora, tq), BF16),
            pltpu.VMEM((n_heads, LANES, tq), BF16),
            pltpu.VMEM((n_heads, SUBLANES, tq), F32),
            pltpu.VMEM((seq, tq), jnp.int32),
            pltpu.VMEM((seq, tq), F32),
            pltpu.VMEM((n_heads, kv_lora, tq), BF16),
            pltpu.VMEM((2, seq, tq), F32),
            pltpu.VMEM((2, SUBLANES, tq), F32),
            pltpu.VMEM((2, SUBLANES, tq), F32),
            pltpu.VMEM((2, seq, tq), BF16),
        ],
        compiler_params=_params(2),
        name="dsa_attention",
    )(cqt, widxt, kidx, ckv, ckvt, wuqt, wukt, wqit, wuvt)


def _out_proj_kernel(h_ref, o_ref, w_ref, out_ref):
    out_ref[...] = h_ref[...] + _dot(o_ref[...], w_ref[...])


def _out_proj(h, o, w, *, tm):
    n, d = h.shape
    k = o.shape[1]
    return pl.pallas_call(
        _out_proj_kernel,
        grid=(n // tm,),
        in_specs=[
            pl.BlockSpec((tm, d), lambda i: (i, 0)),
            pl.BlockSpec((tm, k), lambda i: (i, 0)),
            pl.BlockSpec((k, d), lambda i: (0, 0)),
        ],
        out_specs=pl.BlockSpec((tm, d), lambda i: (i, 0)),
        out_shape=jax.ShapeDtypeStruct((n, d), F32),
        compiler_params=_params(1),
        name="attn_out_proj",
    )(h, o, w)


def _pad_last(a, width):
    return jnp.pad(a, [(0, 0)] * (a.ndim - 1) + [(0, width - a.shape[-1])])


def _tile(n, want):
    t = min(n, want)
    assert n % t == 0, (n, t)
    return t


def kernel(x, norm_mix, norm_mlp, mlp_w1, mlp_w2, conv_in, conv_w, conv_out, attn_in,
           q_norm, kv_norm, w_uq, w_uk, w_uv, w_qidx, kidx_ln_g, kidx_ln_b, attn_out,
           final_norm):
    batch, seq, d = x.shape
    depth = norm_mix.shape[0]
    q_lora, n_heads, hd = w_uq.shape[1:]
    kv_lora = w_uk.shape[3]
    idx_heads, idx_dim = w_qidx.shape[2:]
    assert idx_heads == n_heads and idx_dim <= LANES and n_heads <= LANES
    assert n_heads % SCORE_GROUP == 0 and n_heads % 2 == 0 and n_heads >= 4
    topk = min(TOPK_MAX, seq // 4)
    n = batch * seq

    tm = _tile(seq, 512)
    tf = _tile(mlp_w1.shape[2], 2048)
    tc = _tile(d, 512)
    tq = _tile(seq, 256)

    h = x.reshape(n, d)
    row = lambda v: v.reshape(1, -1)
    gf = row(final_norm)
    for i in range(depth):
        j = i // 2
        g = row(norm_mix[i])
        if i % 2 == 0:
            cw = jnp.pad(conv_w[j], ((0, 8 - conv_w.shape[1]), (0, 0)))
            h = _conv(h, g, conv_in[j].astype(BF16), cw, conv_out[j].astype(BF16),
                      seq=seq, tm=tm, tc=tc)
        else:
            o = q_lora + kv_lora
            w = jnp.concatenate([
                attn_in[j][:, :o],
                _pad_last(attn_in[j][:, o:o + idx_dim], LANES),
                _pad_last(attn_in[j][:, o + idx_dim:], LANES)], axis=1).astype(BF16)
            cqt, ckv, ckvt, kidx, widxt = _attn_proj(
                h, g, w, row(q_norm[j]), row(kv_norm[j]),
                _pad_last(row(kidx_ln_g[j]), LANES), _pad_last(row(kidx_ln_b[j]), LANES),
                q_lora=q_lora, kv_lora=kv_lora, idx_dim=idx_dim,
                w_scale=float(idx_heads ** -0.5 * idx_dim ** -0.5), tm=tm)
            wuqt = w_uq[j].reshape(q_lora, n_heads * hd).T.astype(BF16)
            wukt = w_uk[j].transpose(0, 2, 1).astype(BF16)
            wqit = _pad_last(w_qidx[j], LANES).reshape(q_lora, n_heads * LANES).T.astype(BF16)
            wuvt = w_uv[j].transpose(0, 2, 1).astype(BF16)
            oh = _attn(cqt, widxt, kidx, ckv, ckvt, wuqt, wukt, wqit, wuvt,
                       batch=batch, seq=seq, topk=topk, scale=float(hd ** -0.5), tq=tq)
            h = _out_proj(h, oh, attn_out[j].astype(BF16), tm=tm)
        h = _mlp(h, row(norm_mlp[i]), mlp_w1[i].astype(BF16), mlp_w2[i].astype(BF16), gf,
                 final=(i == depth - 1), tm=tm, tf=tf)
    return h.reshape(batch, seq, d)
```

```python
import functools

import jax
import jax.numpy as jnp
from jax import lax
from jax.experimental import pallas as pl
from jax.experimental.pallas import tpu as pltpu

EPS = 1e-6
TOPK_MAX = 256
LANES = 128
SUBLANES = 8
VMEM_LIMIT_BYTES = 56 * 1024 * 1024
INT_MIN = -2 ** 31
LOG2_E = 1.4426950408889634
SCORE_GROUP = 4
REDUCE_SLAB = 64
PIPE_UNROLL = 2

F32 = jnp.float32
BF16 = jnp.bfloat16


def _params(n_axes):
    return pltpu.CompilerParams(
        dimension_semantics=("arbitrary",) * n_axes,
        vmem_limit_bytes=VMEM_LIMIT_BYTES)


def _rms(x, g):
    return x * lax.rsqrt(jnp.mean(x * x, axis=-1, keepdims=True) + EPS) * g


def _dot(a, b):
    return jnp.dot(a, b, preferred_element_type=F32)


def _reduce_rows(op, x):
    rows, cols = x.shape
    if rows % REDUCE_SLAB == 0 and rows > REDUCE_SLAB:
        x = op(x.reshape(rows // REDUCE_SLAB, REDUCE_SLAB, cols), axis=0)
    return op(x, axis=0, keepdims=True)


def _mlp_kernel(h_ref, g_ref, w1_ref, w2_ref, gf_ref, o_ref, hn_ref, *, final):
    j = pl.program_id(1)

    @pl.when(j == 0)
    def _():
        h = h_ref[...]
        hn_ref[...] = _rms(h, g_ref[...]).astype(BF16)
        o_ref[...] = h

    a = jnp.maximum(_dot(hn_ref[...], w1_ref[...]), 0.0)
    o_ref[...] += _dot((a * a).astype(BF16), w2_ref[...])

    if final:
        @pl.when(j == pl.num_programs(1) - 1)
        def _():
            o_ref[...] = _rms(o_ref[...], gf_ref[...])


def _mlp(h, g, w1, w2, gf, *, final, tm, tf):
    n, d = h.shape
    ff = w1.shape[1]
    return pl.pallas_call(
        functools.partial(_mlp_kernel, final=final),
        grid=(n // tm, ff // tf),
        in_specs=[
            pl.BlockSpec((tm, d), lambda i, j: (i, 0)),
            pl.BlockSpec((1, d), lambda i, j: (0, 0)),
            pl.BlockSpec((d, tf), lambda i, j: (0, j)),
            pl.BlockSpec((tf, d), lambda i, j: (j, 0)),
            pl.BlockSpec((1, d), lambda i, j: (0, 0)),
        ],
        out_specs=pl.BlockSpec((tm, d), lambda i, j: (i, 0)),
        out_shape=jax.ShapeDtypeStruct((n, d), F32),
        scratch_shapes=[pltpu.VMEM((tm, d), BF16)],
        compiler_params=_params(2),
        name="sqrelu_mlp",
    )(h, g, w1, w2, gf)


def _conv_kernel(h_ref, g_ref, wb_ref, wc_ref, wx_ref, cw_ref, wo_ref, o_ref,
                 hn_ref, halo_ref, *, tiles_per_seq):
    i = pl.program_id(0)
    j = pl.program_id(1)

    @pl.when(j == 0)
    def _():
        h = h_ref[...]
        hn_ref[...] = _rms(h, g_ref[...]).astype(BF16)
        o_ref[...] = h

    hn = hn_ref[...]
    bg = _dot(hn, wb_ref[...])
    z = _dot(hn, wc_ref[...]) * _dot(hn, wx_ref[...])
    tm = z.shape[0]

    @pl.when(i % tiles_per_seq == 0)
    def _():
        halo_ref[j] = jnp.zeros(halo_ref.shape[1:], F32)

    prev = halo_ref[j]
    halo_ref[j] = z[tm - 8:, :]
    p1 = prev[7:8, :]
    p2 = prev[6:7, :]

    row = lax.broadcasted_iota(jnp.int32, (tm, 1), 0)
    z1 = jnp.where(row == 0, p1, pltpu.roll(z, 1, 0))
    z2 = jnp.where(row == 0, p2, jnp.where(row == 1, p1, pltpu.roll(z, 2, 0)))
    cw = cw_ref[...]
    zc = cw[0:1, :] * z2 + cw[1:2, :] * z1 + cw[2:3, :] * z
    o_ref[...] += _dot((bg * zc).astype(BF16), wo_ref[...])


def _conv(h, g, w_in, cw, w_out, *, seq, tm, tc):
    n, d = h.shape
    nc = d // tc
    return pl.pallas_call(
        functools.partial(_conv_kernel, tiles_per_seq=seq // tm),
        grid=(n // tm, nc),
        in_specs=[
            pl.BlockSpec((tm, d), lambda i, j: (i, 0)),
            pl.BlockSpec((1, d), lambda i, j: (0, 0)),
            pl.BlockSpec((d, tc), lambda i, j: (0, j)),
            pl.BlockSpec((d, tc), lambda i, j: (0, j + nc)),
            pl.BlockSpec((d, tc), lambda i, j: (0, j + 2 * nc)),
            pl.BlockSpec((8, tc), lambda i, j: (0, j)),
            pl.BlockSpec((tc, d), lambda i, j: (j, 0)),
        ],
        out_specs=pl.BlockSpec((tm, d), lambda i, j: (i, 0)),
        out_shape=jax.ShapeDtypeStruct((n, d), F32),
        scratch_shapes=[pltpu.VMEM((tm, d), BF16), pltpu.VMEM((nc, 8, tc), F32)],
        compiler_params=_params(2),
        name="short_conv",
    )(h, g, w_in, w_in, w_in, cw, w_out)


def _attn_proj_kernel(h_ref, g_ref, w_ref, qg_ref, kvg_ref, lng_ref, lnb_ref,
                      cqt_ref, ckv_ref, ckvt_ref, kidx_ref, widxt_ref,
                      *, q_lora, kv_lora, idx_dim, w_scale):
    hn = _rms(h_ref[...], g_ref[...]).astype(BF16)
    proj = _dot(hn, w_ref[...])
    cqt_ref[...] = _rms(proj[:, :q_lora], qg_ref[...]).T.astype(BF16)
    ckv = _rms(proj[:, q_lora:q_lora + kv_lora], kvg_ref[...])
    ckv_ref[...] = ckv.astype(BF16)
    ckvt_ref[...] = ckv.T.astype(BF16)

    o = q_lora + kv_lora
    kx = proj[:, o:o + LANES]
    valid = lax.broadcasted_iota(jnp.int32, (1, LANES), 1) < idx_dim
    mu = jnp.sum(kx, axis=-1, keepdims=True) * (1.0 / idx_dim)
    dlt = jnp.where(valid, kx - mu, 0.0)
    var = jnp.sum(dlt * dlt, axis=-1, keepdims=True) * (1.0 / idx_dim)
    kidx_ref[...] = (dlt * lax.rsqrt(var + EPS) * lng_ref[...] + lnb_ref[...]).astype(BF16)
    widxt_ref[...] = (proj[:, o + LANES:o + 2 * LANES] * w_scale).T


def _attn_proj(h, g, w, qg, kvg, lng, lnb, *, q_lora, kv_lora, idx_dim, w_scale, tm):
    n, d = h.shape
    cols = w.shape[1]
    row = lambda c: pl.BlockSpec((1, c), lambda i: (0, 0))
    return pl.pallas_call(
        functools.partial(_attn_proj_kernel, q_lora=q_lora, kv_lora=kv_lora,
                          idx_dim=idx_dim, w_scale=w_scale),
        grid=(n // tm,),
        in_specs=[
            pl.BlockSpec((tm, d), lambda i: (i, 0)),
            row(d),
            pl.BlockSpec((d, cols), lambda i: (0, 0)),
            row(q_lora), row(kv_lora), row(LANES), row(LANES),
        ],
        out_specs=[
            pl.BlockSpec((q_lora, tm), lambda i: (0, i)),
            pl.BlockSpec((tm, kv_lora), lambda i: (i, 0)),
            pl.BlockSpec((kv_lora, tm), lambda i: (0, i)),
            pl.BlockSpec((tm, LANES), lambda i: (i, 0)),
            pl.BlockSpec((LANES, tm), lambda i: (0, i)),
        ],
        out_shape=[
            jax.ShapeDtypeStruct((q_lora, n), BF16),
            jax.ShapeDtypeStruct((n, kv_lora), BF16),
            jax.ShapeDtypeStruct((kv_lora, n), BF16),
            jax.ShapeDtypeStruct((n, LANES), BF16),
            jax.ShapeDtypeStruct((LANES, n), F32),
        ],
        compiler_params=_params(1),
        name="attn_proj",
    )(h, g, w, qg, kvg, lng, lnb)


def _count(pred):
    return _reduce_rows(jnp.sum, pred.astype(F32))


def _attn_select(c, kidx_ref, qi_s, w_s, key_s, bias_s, thr_s, tie_s, *, n_heads, topk):
    seq, tq = key_s.shape
    s = (c + 1) * tq
    kpos = lax.broadcasted_iota(jnp.int32, (s, 1), 0)
    qpos = c * tq + lax.broadcasted_iota(jnp.int32, (1, tq), 1)
    causal = kpos <= qpos

    if s <= topk:
        bias_s[:s, :] = jnp.where(causal, 0.0, -jnp.inf)
        tie_s[0] = 0
        return

    def head_score(h):
        return w_s[h][0:1, :] * jnp.maximum(_dot(kidx_ref[:s, :], qi_s[h]), 0.0)

    def score_body(g, carry):
        h = g * SCORE_GROUP
        part = head_score(h)
        for d in range(1, SCORE_GROUP):
            part = part + head_score(h + d)
        bias_s[:s, :] += part
        return carry
    bias_s[:s, :] = jnp.zeros((s, tq), F32)
    lax.fori_loop(0, n_heads // SCORE_GROUP, score_body, 0)

    bits = pltpu.bitcast(bias_s[:s, :], jnp.int32)
    key = bits ^ ((bits >> 31) & 0x7FFFFFFF)
    key_s[:s, :] = jnp.where(causal, key, INT_MIN)
    if s < seq:
        key_s[s:, :] = jnp.full((seq - s, tq), INT_MIN, jnp.int32)

    def select_body(i, thr_u):
        cand_u = thr_u | jnp.left_shift(jnp.int32(1), 31 - i)
        ok = _count(key_s[:s, :] >= (cand_u ^ INT_MIN)) >= topk
        return jnp.where(ok, cand_u, thr_u)
    thr_u = lax.fori_loop(0, 32, select_body, jnp.zeros((1, tq), jnp.int32))
    thr = thr_u ^ INT_MIN
    thr_s[...] = jnp.broadcast_to(thr, thr_s.shape)

    ge = key_s[:s, :] >= thr
    if c * tq < topk:
        ge = ge & causal
    bias_s[:s, :] = jnp.where(ge, 0.0, -jnp.inf)
    tie_s[0] = (jnp.max(_count(ge)) > topk).astype(jnp.int32)


def _attn_ties(t0, key_s, bias_s, thr_s, *, topk):
    seq, tq = key_s.shape
    kpos = lax.broadcasted_iota(jnp.int32, (seq, 1), 0)
    qpos = t0 + lax.broadcasted_iota(jnp.int32, (1, tq), 1)
    causal = kpos <= qpos
    key = key_s[...]
    thr = thr_s[0:1, :]
    gt = (key > thr) & causal
    eq = (key == thr) & causal
    keep = topk - _count(gt)
    nbits = seq.bit_length()

    def cut_body(i, cut):
        cand = cut | jnp.left_shift(jnp.int32(1), nbits - 1 - i)
        ok = _count(eq & (kpos < cand)) <= keep
        return jnp.where(ok, cand, cut)
    cut = lax.fori_loop(0, nbits, cut_body, jnp.zeros((1, tq), jnp.int32))
    bias_s[...] = jnp.where(gt | (eq & (kpos < cut)), 0.0, -jnp.inf)


def _attn_heads(c, ckv_ref, ckvt_ref, ql_s, bias_s, ol_s, lg_s, m_s, d_s, e_s, *, n_heads):
    tq = bias_s.shape[1]
    s = (c + 1) * tq

    def logits(q, b):
        lg = _dot(ckv_ref[:s, :], q) + bias_s[:s, :]
        lg_s[b, :s, :] = lg
        m_s[b] = jnp.broadcast_to(_reduce_rows(jnp.max, lg), (SUBLANES, tq))

    def softmax(b):
        e = jnp.exp2(lg_s[b, :s, :] - m_s[b][0:1, :])
        d_s[b] = jnp.broadcast_to(_reduce_rows(jnp.sum, e), (SUBLANES, tq))
        e_s[b, :s, :] = e.astype(BF16)

    def pv(b):
        return (_dot(ckvt_ref[:, :s], e_s[b, :s, :]) / d_s[b][0:1, :]).astype(BF16)

    def steps(t0, count):
        qs = [ql_s[t0 + u] for u in range(count)]
        outs = []
        for u in range(count):
            outs.append(pv(u % 2))
            softmax((u + 1) % 2)
            logits(qs[u], u % 2)
        for u in range(count):
            ol_s[t0 + u - 2] = outs[u]

    logits(ql_s[0], 0)
    softmax(0)
    logits(ql_s[1], 1)

    n_steady = n_heads - 2
    peeled = n_steady % PIPE_UNROLL
    if peeled:
        steps(2, peeled)
    if n_steady // PIPE_UNROLL > 1:
        def pipe_body(p, carry):
            steps(2 + peeled + PIPE_UNROLL * p, PIPE_UNROLL)
            return carry
        lax.fori_loop(0, n_steady // PIPE_UNROLL, pipe_body, 0)
    elif n_steady // PIPE_UNROLL == 1:
        steps(2 + peeled, PIPE_UNROLL)

    oa = pv(0)
    softmax(1)
    ol_s[n_heads - 2] = oa
    ol_s[n_heads - 1] = pv(1)


def _attn_kernel(cqt_ref, widxt_ref, kidx_ref, ckv_ref, ckvt_ref,
                 wuqt_ref, wukt_ref, wqit_ref, wuvt_ref, o_ref,
                 ql_s, qi_s, w_s, key_s, bias_s, ol_s, lg_s, m_s, d_s, e_s, thr_s, tie_s,
                 *, n_heads, topk, qscale):
    seq, tq = key_s.shape
    hd = wukt_ref.shape[2]
    vd = wuvt_ref.shape[1]
    cqt = cqt_ref[...]
    q_all = _dot(wuqt_ref[...], cqt).astype(BF16)
    qi_all = _dot(wqit_ref[...], cqt).astype(BF16)
    widxt = widxt_ref[...]
    for h in range(n_heads):
        q_lat = _dot(wukt_ref[h], q_all[h * hd:(h + 1) * hd, :]) * qscale
        ql_s[h] = q_lat.astype(BF16)
        qi_s[h] = qi_all[h * LANES:(h + 1) * LANES, :]
        w_s[h] = jnp.broadcast_to(widxt[h:h + 1, :], (SUBLANES, tq))

    qtile = pl.program_id(1)
    for c in range(seq // tq):
        pl.when(qtile == c)(functools.partial(
            _attn_select, c, kidx_ref, qi_s, w_s, key_s, bias_s, thr_s, tie_s,
            n_heads=n_heads, topk=topk))
    pl.when(tie_s[0] != 0)(functools.partial(
        _attn_ties, qtile * tq, key_s, bias_s, thr_s, topk=topk))
    for c in range(seq // tq):
        pl.when(qtile == c)(functools.partial(
            _attn_heads, c, ckv_ref, ckvt_ref, ql_s, bias_s, ol_s, lg_s, m_s, d_s, e_s,
            n_heads=n_heads))

    for h in range(n_heads):
        o_ref[:, h * vd:(h + 1) * vd] = _dot(wuvt_ref[h], ol_s[h]).T.astype(BF16)


def _attn(cqt, widxt, kidx, ckv, ckvt, wuqt, wukt, wqit, wuvt, *, batch, seq, topk, scale, tq):
    q_lora, n = cqt.shape
    n_heads, kv_lora, hd = wukt.shape
    vd = wuvt.shape[1]
    nq = seq // tq
    const2 = lambda a: pl.BlockSpec(a.shape, lambda b, q: (0, 0))
    const3 = lambda a: pl.BlockSpec(a.shape, lambda b, q: (0, 0, 0))
    return pl.pallas_call(
        functools.partial(_attn_kernel, n_heads=n_heads, topk=topk,
                          qscale=scale * LOG2_E),
        grid=(batch, nq),
        in_specs=[
            pl.BlockSpec((q_lora, tq), lambda b, q: (0, b * nq + q)),
            pl.BlockSpec((LANES, tq), lambda b, q: (0, b * nq + q)),
            pl.BlockSpec((seq, LANES), lambda b, q: (b, 0)),
            pl.BlockSpec((seq, kv_lora), lambda b, q: (b, 0)),
            pl.BlockSpec((kv_lora, seq), lambda b, q: (0, b)),
            const2(wuqt), const3(wukt), const2(wqit), const3(wuvt),
        ],
        out_specs=pl.BlockSpec((tq, n_heads * vd), lambda b, q: (b * nq + q, 0)),
        out_shape=jax.ShapeDtypeStruct((n, n_heads * vd), BF16),
        scratch_shapes=[
            pltpu.VMEM((n_heads, kv_lora, tq), BF16),
            pltpu.VMEM((n_heads, LANES, tq), BF16),
            pltpu.VMEM((n_heads, SUBLANES, tq), F32),
            pltpu.VMEM((seq, tq), jnp.int32),
            pltpu.VMEM((seq, tq), F32),
            pltpu.VMEM((n_heads, kv_lora, tq), BF16),
            pltpu.VMEM((2, seq, tq), F32),
            pltpu.VMEM((2, SUBLANES, tq), F32),
            pltpu.VMEM((2, SUBLANES, tq), F32),
            pltpu.VMEM((2, seq, tq), BF16),
            pltpu.VMEM((SUBLANES, tq), jnp.int32),
            pltpu.SMEM((1,), jnp.int32),
        ],
        compiler_params=_params(2),
        name="dsa_attention",
    )(cqt, widxt, kidx, ckv, ckvt, wuqt, wukt, wqit, wuvt)


def _out_proj_kernel(h_ref, o_ref, w_ref, out_ref):
    out_ref[...] = h_ref[...] + _dot(o_ref[...], w_ref[...])


def _out_proj(h, o, w, *, tm):
    n, d = h.shape
    k = o.shape[1]
    return pl.pallas_call(
        _out_proj_kernel,
        grid=(n // tm,),
        in_specs=[
            pl.BlockSpec((tm, d), lambda i: (i, 0)),
            pl.BlockSpec((tm, k), lambda i: (i, 0)),
            pl.BlockSpec((k, d), lambda i: (0, 0)),
        ],
        out_specs=pl.BlockSpec((tm, d), lambda i: (i, 0)),
        out_shape=jax.ShapeDtypeStruct((n, d), F32),
        compiler_params=_params(1),
        name="attn_out_proj",
    )(h, o, w)


def _pad_last(a, width):
    return jnp.pad(a, [(0, 0)] * (a.ndim - 1) + [(0, width - a.shape[-1])])


def _tile(n, want):
    t = min(n, want)
    assert n % t == 0, (n, t)
    return t


def kernel(x, norm_mix, norm_mlp, mlp_w1, mlp_w2, conv_in, conv_w, conv_out, attn_in,
           q_norm, kv_norm, w_uq, w_uk, w_uv, w_qidx, kidx_ln_g, kidx_ln_b, attn_out,
           final_norm):
    batch, seq, d = x.shape
    depth = norm_mix.shape[0]
    q_lora, n_heads, hd = w_uq.shape[1:]
    kv_lora = w_uk.shape[3]
    idx_heads, idx_dim = w_qidx.shape[2:]
    assert idx_heads == n_heads and idx_dim <= LANES and n_heads <= LANES
    assert n_heads % SCORE_GROUP == 0 and n_heads % 2 == 0 and n_heads >= 4
    topk = min(TOPK_MAX, seq // 4)
    n = batch * seq

    tm = _tile(seq, 512)
    tf = _tile(mlp_w1.shape[2], 2048)
    tc = _tile(d, 512)
    tq = _tile(seq, 256)

    h = x.reshape(n, d)
    row = lambda v: v.reshape(1, -1)
    gf = row(final_norm)
    for i in range(depth):
        j = i // 2
        g = row(norm_mix[i])
        if i % 2 == 0:
            cw = jnp.pad(conv_w[j], ((0, 8 - conv_w.shape[1]), (0, 0)))
            h = _conv(h, g, conv_in[j].astype(BF16), cw, conv_out[j].astype(BF16),
                      seq=seq, tm=tm, tc=tc)
        else:
            o = q_lora + kv_lora
            w = jnp.concatenate([
                attn_in[j][:, :o],
                _pad_last(attn_in[j][:, o:o + idx_dim], LANES),
                _pad_last(attn_in[j][:, o + idx_dim:], LANES)], axis=1).astype(BF16)
            cqt, ckv, ckvt, kidx, widxt = _attn_proj(
                h, g, w, row(q_norm[j]), row(kv_norm[j]),
                _pad_last(row(kidx_ln_g[j]), LANES), _pad_last(row(kidx_ln_b[j]), LANES),
                q_lora=q_lora, kv_lora=kv_lora, idx_dim=idx_dim,
                w_scale=float(idx_heads ** -0.5 * idx_dim ** -0.5), tm=tm)
            wuqt = w_uq[j].reshape(q_lora, n_heads * hd).T.astype(BF16)
            wukt = w_uk[j].transpose(0, 2, 1).astype(BF16)
            wqit = _pad_last(w_qidx[j], LANES).reshape(q_lora, n_heads * LANES).T.astype(BF16)
            wuvt = w_uv[j].transpose(0, 2, 1).astype(BF16)
            oh = _attn(cqt, widxt, kidx, ckv, ckvt, wuqt, wukt, wqit, wuvt,
                       batch=batch, seq=seq, topk=topk, scale=float(hd ** -0.5), tq=tq)
            h = _out_proj(h, oh, attn_out[j].astype(BF16), tm=tm)
        h = _mlp(h, row(norm_mlp[i]), mlp_w1[i].astype(BF16), mlp_w2[i].astype(BF16), gf,
                 final=(i == depth - 1), tm=tm, tf=tf)
    return h.reshape(batch, seq, d)
```

```python
import functools

import jax
import jax.numpy as jnp
from jax import lax
from jax.experimental import pallas as pl
from jax.experimental.pallas import tpu as pltpu

EPS = 1e-6
TOPK_MAX = 256
LANES = 128
SUBLANES = 8
VMEM_LIMIT_BYTES = 56 * 1024 * 1024
INT_MIN = -2 ** 31
LOG2_E = 1.4426950408889634
SCORE_GROUP = 4
REDUCE_SLAB = 64
PIPE_UNROLL = 2

F32 = jnp.float32
BF16 = jnp.bfloat16


def _params(n_axes):
    return pltpu.CompilerParams(
        dimension_semantics=("arbitrary",) * n_axes,
        vmem_limit_bytes=VMEM_LIMIT_BYTES)


def _rms(x, g):
    return x * lax.rsqrt(jnp.mean(x * x, axis=-1, keepdims=True) + EPS) * g


def _dot(a, b):
    return jnp.dot(a, b, preferred_element_type=F32)


def _reduce_rows(op, x):
    rows, cols = x.shape
    if rows % REDUCE_SLAB == 0 and rows > REDUCE_SLAB:
        x = op(x.reshape(rows // REDUCE_SLAB, REDUCE_SLAB, cols), axis=0)
    return op(x, axis=0, keepdims=True)


def _mlp_kernel(h_ref, g_ref, w1_ref, w2_ref, gf_ref, o_ref, hn_ref, *, final):
    j = pl.program_id(1)

    @pl.when(j == 0)
    def _():
        h = h_ref[...]
        hn_ref[...] = _rms(h, g_ref[...]).astype(BF16)
        o_ref[...] = h

    a = jnp.maximum(_dot(hn_ref[...], w1_ref[...]), 0.0)
    o_ref[...] += _dot((a * a).astype(BF16), w2_ref[...])

    if final:
        @pl.when(j == pl.num_programs(1) - 1)
        def _():
            o_ref[...] = _rms(o_ref[...], gf_ref[...])


def _mlp(h, g, w1, w2, gf, *, final, tm, tf):
    n, d = h.shape
    ff = w1.shape[1]
    return pl.pallas_call(
        functools.partial(_mlp_kernel, final=final),
        grid=(n // tm, ff // tf),
        in_specs=[
            pl.BlockSpec((tm, d), lambda i, j: (i, 0)),
            pl.BlockSpec((1, d), lambda i, j: (0, 0)),
            pl.BlockSpec((d, tf), lambda i, j: (0, j)),
            pl.BlockSpec((tf, d), lambda i, j: (j, 0)),
            pl.BlockSpec((1, d), lambda i, j: (0, 0)),
        ],
        out_specs=pl.BlockSpec((tm, d), lambda i, j: (i, 0)),
        out_shape=jax.ShapeDtypeStruct((n, d), F32),
        scratch_shapes=[pltpu.VMEM((tm, d), BF16)],
        compiler_params=_params(2),
        name="sqrelu_mlp",
    )(h, g, w1, w2, gf)


def _conv_kernel(h_ref, g_ref, wb_ref, wc_ref, wx_ref, cw_ref, wo_ref, o_ref,
                 hn_ref, halo_ref, *, tiles_per_seq):
    i = pl.program_id(0)
    j = pl.program_id(1)

    @pl.when(j == 0)
    def _():
        h = h_ref[...]
        hn_ref[...] = _rms(h, g_ref[...]).astype(BF16)
        o_ref[...] = h

    hn = hn_ref[...]
    bg = _dot(hn, wb_ref[...])
    z = _dot(hn, wc_ref[...]) * _dot(hn, wx_ref[...])
    tm = z.shape[0]

    @pl.when(i % tiles_per_seq == 0)
    def _():
        halo_ref[j] = jnp.zeros(halo_ref.shape[1:], F32)

    prev = halo_ref[j]
    halo_ref[j] = z[tm - 8:, :]
    p1 = prev[7:8, :]
    p2 = prev[6:7, :]

    row = lax.broadcasted_iota(jnp.int32, (tm, 1), 0)
    z1 = jnp.where(row == 0, p1, pltpu.roll(z, 1, 0))
    z2 = jnp.where(row == 0, p2, jnp.where(row == 1, p1, pltpu.roll(z, 2, 0)))
    cw = cw_ref[...]
    zc = cw[0:1, :] * z2 + cw[1:2, :] * z1 + cw[2:3, :] * z
    o_ref[...] += _dot((bg * zc).astype(BF16), wo_ref[...])


def _conv(h, g, w_in, cw, w_out, *, seq, tm, tc):
    n, d = h.shape
    nc = d // tc
    return pl.pallas_call(
        functools.partial(_conv_kernel, tiles_per_seq=seq // tm),
        grid=(n // tm, nc),
        in_specs=[
            pl.BlockSpec((tm, d), lambda i, j: (i, 0)),
            pl.BlockSpec((1, d), lambda i, j: (0, 0)),
            pl.BlockSpec((d, tc), lambda i, j: (0, j)),
            pl.BlockSpec((d, tc), lambda i, j: (0, j + nc)),
            pl.BlockSpec((d, tc), lambda i, j: (0, j + 2 * nc)),
            pl.BlockSpec((8, tc), lambda i, j: (0, j)),
            pl.BlockSpec((tc, d), lambda i, j: (j, 0)),
        ],
        out_specs=pl.BlockSpec((tm, d), lambda i, j: (i, 0)),
        out_shape=jax.ShapeDtypeStruct((n, d), F32),
        scratch_shapes=[pltpu.VMEM((tm, d), BF16), pltpu.VMEM((nc, 8, tc), F32)],
        compiler_params=_params(2),
        name="short_conv",
    )(h, g, w_in, w_in, w_in, cw, w_out)


def _attn_proj_kernel(h_ref, g_ref, w_ref, qg_ref, kvg_ref, lng_ref, lnb_ref,
                      cqt_ref, ckv_ref, ckvt_ref, kidx_ref, widxt_ref,
                      *, q_lora, kv_lora, idx_dim, w_scale):
    hn = _rms(h_ref[...], g_ref[...]).astype(BF16)
    proj = _dot(hn, w_ref[...])
    cqt_ref[...] = _rms(proj[:, :q_lora], qg_ref[...]).T.astype(BF16)
    ckv = _rms(proj[:, q_lora:q_lora + kv_lora], kvg_ref[...])
    ckv_ref[...] = ckv.astype(BF16)
    ckvt_ref[...] = ckv.T.astype(BF16)

    o = q_lora + kv_lora
    kx = proj[:, o:o + LANES]
    valid = lax.broadcasted_iota(jnp.int32, (1, LANES), 1) < idx_dim
    mu = jnp.sum(kx, axis=-1, keepdims=True) * (1.0 / idx_dim)
    dlt = jnp.where(valid, kx - mu, 0.0)
    var = jnp.sum(dlt * dlt, axis=-1, keepdims=True) * (1.0 / idx_dim)
    kidx_ref[...] = (dlt * lax.rsqrt(var + EPS) * lng_ref[...] + lnb_ref[...]).astype(BF16)
    widxt_ref[...] = (proj[:, o + LANES:o + 2 * LANES] * w_scale).T


def _attn_proj(h, g, w, qg, kvg, lng, lnb, *, q_lora, kv_lora, idx_dim, w_scale, tm):
    n, d = h.shape
    cols = w.shape[1]
    row = lambda c: pl.BlockSpec((1, c), lambda i: (0, 0))
    return pl.pallas_call(
        functools.partial(_attn_proj_kernel, q_lora=q_lora, kv_lora=kv_lora,
                          idx_dim=idx_dim, w_scale=w_scale),
        grid=(n // tm,),
        in_specs=[
            pl.BlockSpec((tm, d), lambda i: (i, 0)),
            row(d),
            pl.BlockSpec((d, cols), lambda i: (0, 0)),
            row(q_lora), row(kv_lora), row(LANES), row(LANES),
        ],
        out_specs=[
            pl.BlockSpec((q_lora, tm), lambda i: (0, i)),
            pl.BlockSpec((tm, kv_lora), lambda i: (i, 0)),
            pl.BlockSpec((kv_lora, tm), lambda i: (0, i)),
            pl.BlockSpec((tm, LANES), lambda i: (i, 0)),
            pl.BlockSpec((LANES, tm), lambda i: (0, i)),
        ],
        out_shape=[
            jax.ShapeDtypeStruct((q_lora, n), BF16),
            jax.ShapeDtypeStruct((n, kv_lora), BF16),
            jax.ShapeDtypeStruct((kv_lora, n), BF16),
            jax.ShapeDtypeStruct((n, LANES), BF16),
            jax.ShapeDtypeStruct((LANES, n), F32),
        ],
        compiler_params=_params(1),
        name="attn_proj",
    )(h, g, w, qg, kvg, lng, lnb)


def _count(pred):
    return _reduce_rows(jnp.sum, pred.astype(F32))


def _attn_select(c, kidx_ref, qi_s, w_s, key_s, bias_s, thr_s, tie_s, *, n_heads, topk):
    seq, tq = key_s.shape
    s = (c + 1) * tq
    kpos = lax.broadcasted_iota(jnp.int32, (s, 1), 0)
    qpos = c * tq + lax.broadcasted_iota(jnp.int32, (1, tq), 1)
    causal = kpos <= qpos

    if s <= topk:
        bias_s[:s, :] = jnp.where(causal, 0.0, -jnp.inf)
        tie_s[0] = 0
        return

    def head_score(h):
        return w_s[h][0:1, :] * jnp.maximum(_dot(kidx_ref[:s, :], qi_s[h]), 0.0)

    def score_body(g, carry):
        h = g * SCORE_GROUP
        part = head_score(h)
        for d in range(1, SCORE_GROUP):
            part = part + head_score(h + d)
        bias_s[:s, :] += part
        return carry
    bias_s[:s, :] = jnp.zeros((s, tq), F32)
    lax.fori_loop(0, n_heads // SCORE_GROUP, score_body, 0)

    bits = pltpu.bitcast(bias_s[:s, :], jnp.int32)
    key = bits ^ ((bits >> 31) & 0x7FFFFFFF)
    key_s[:s, :] = jnp.where(causal, key, INT_MIN)
    if s < seq:
        key_s[s:, :] = jnp.full((seq - s, tq), INT_MIN, jnp.int32)

    def select_body(i, thr_u):
        cand_u = thr_u | jnp.left_shift(jnp.int32(1), 31 - i)
        ok = _count(key_s[:s, :] >= (cand_u ^ INT_MIN)) >= topk
        return jnp.where(ok, cand_u, thr_u)
    thr_u = lax.fori_loop(0, 32, select_body, jnp.zeros((1, tq), jnp.int32))
    thr = thr_u ^ INT_MIN
    thr_s[...] = jnp.broadcast_to(thr, thr_s.shape)

    ge = key_s[:s, :] >= thr
    if c * tq < topk:
        ge = ge & causal
    bias_s[:s, :] = jnp.where(ge, 0.0, -jnp.inf)
    tie_s[0] = (jnp.max(_count(ge)) > topk).astype(jnp.int32)


def _attn_ties(t0, key_s, bias_s, thr_s, *, topk):
    seq, tq = key_s.shape
    kpos = lax.broadcasted_iota(jnp.int32, (seq, 1), 0)
    qpos = t0 + lax.broadcasted_iota(jnp.int32, (1, tq), 1)
    causal = kpos <= qpos
    key = key_s[...]
    thr = thr_s[0:1, :]
    gt = (key > thr) & causal
    eq = (key == thr) & causal
    keep = topk - _count(gt)
    nbits = seq.bit_length()

    def cut_body(i, cut):
        cand = cut | jnp.left_shift(jnp.int32(1), nbits - 1 - i)
        ok = _count(eq & (kpos < cand)) <= keep
        return jnp.where(ok, cand, cut)
    cut = lax.fori_loop(0, nbits, cut_body, jnp.zeros((1, tq), jnp.int32))
    bias_s[...] = jnp.where(gt | (eq & (kpos < cut)), 0.0, -jnp.inf)


def _attn_heads(c, ckv_ref, ckvt_ref, ql_s, bias_s, ol_s, lg_s, m_s, d_s, e_s, *, n_heads):
    tq = bias_s.shape[1]
    s = (c + 1) * tq

    def logits(q, b):
        lg = _dot(ckv_ref[:s, :], q) + bias_s[:s, :]
        lg_s[b, :s, :] = lg
        m_s[b] = jnp.broadcast_to(_reduce_rows(jnp.max, lg), (SUBLANES, tq))

    def softmax(b):
        e = jnp.exp2(lg_s[b, :s, :] - m_s[b][0:1, :])
        d_s[b] = jnp.broadcast_to(_reduce_rows(jnp.sum, e), (SUBLANES, tq))
        e_s[b, :s, :] = e.astype(BF16)

    def pv(b):
        return (_dot(ckvt_ref[:, :s], e_s[b, :s, :]) / d_s[b][0:1, :]).astype(BF16)

    def steps(t0, count):
        qs = [ql_s[t0 + u] for u in range(count)]
        outs = []
        for u in range(count):
            outs.append(pv(u % 2))
            softmax((u + 1) % 2)
            logits(qs[u], u % 2)
        for u in range(count):
            ol_s[t0 + u - 2] = outs[u]

    logits(ql_s[0], 0)
    softmax(0)
    logits(ql_s[1], 1)

    n_steady = n_heads - 2
    peeled = n_steady % PIPE_UNROLL
    if peeled:
        steps(2, peeled)
    if n_steady // PIPE_UNROLL > 1:
        def pipe_body(p, carry):
            steps(2 + peeled + PIPE_UNROLL * p, PIPE_UNROLL)
            return carry
        lax.fori_loop(0, n_steady // PIPE_UNROLL, pipe_body, 0)
    elif n_steady // PIPE_UNROLL == 1:
        steps(2 + peeled, PIPE_UNROLL)

    oa = pv(0)
    softmax(1)
    ol_s[n_heads - 2] = oa
    ol_s[n_heads - 1] = pv(1)


def _attn_kernel(cqt_ref, widxt_ref, kidx_ref, ckv_ref, ckvt_ref,
                 wuqt_ref, wukt_ref, wqit_ref, wuvt_ref, o_ref,
                 ql_s, qi_s, w_s, key_s, bias_s, ol_s, lg_s, m_s, d_s, e_s, thr_s, tie_s,
                 *, n_heads, topk, qscale):
    seq, tq = key_s.shape
    hd = wukt_ref.shape[2]
    vd = wuvt_ref.shape[1]
    cqt = cqt_ref[...]
    q_all = _dot(wuqt_ref[...], cqt).astype(BF16)
    qi_all = _dot(wqit_ref[...], cqt).astype(BF16)
    widxt = widxt_ref[...]
    for h in range(n_heads):
        q_lat = _dot(wukt_ref[h], q_all[h * hd:(h + 1) * hd, :]) * qscale
        ql_s[h] = q_lat.astype(BF16)
        qi_s[h] = qi_all[h * LANES:(h + 1) * LANES, :]
        w_s[h] = jnp.broadcast_to(widxt[h:h + 1, :], (SUBLANES, tq))

    qtile = pl.program_id(1)
    for c in range(seq // tq):
        pl.when(qtile == c)(functools.partial(
            _attn_select, c, kidx_ref, qi_s, w_s, key_s, bias_s, thr_s, tie_s,
            n_heads=n_heads, topk=topk))
    pl.when(tie_s[0] != 0)(functools.partial(
        _attn_ties, qtile * tq, key_s, bias_s, thr_s, topk=topk))
    for c in range(seq // tq):
        pl.when(qtile == c)(functools.partial(
            _attn_heads, c, ckv_ref, ckvt_ref, ql_s, bias_s, ol_s, lg_s, m_s, d_s, e_s,
            n_heads=n_heads))

    for h in range(n_heads):
        o_ref[:, h * vd:(h + 1) * vd] = _dot(wuvt_ref[h], ol_s[h]).T.astype(BF16)


def _attn(cqt, widxt, kidx, ckv, ckvt, wuqt, wukt, wqit, wuvt, *, batch, seq, topk, scale, tq):
    q_lora, n = cqt.shape
    n_heads, kv_lora, hd = wukt.shape
    vd = wuvt.shape[1]
    nq = seq // tq
    const2 = lambda a: pl.BlockSpec(a.shape, lambda b, q: (0, 0))
    const3 = lambda a: pl.BlockSpec(a.shape, lambda b, q: (0, 0, 0))
    return pl.pallas_call(
        functools.partial(_attn_kernel, n_heads=n_heads, topk=topk,
                          qscale=scale * LOG2_E),
        grid=(batch, nq),
        in_specs=[
            pl.BlockSpec((q_lora, tq), lambda b, q: (0, b * nq + q)),
            pl.BlockSpec((LANES, tq), lambda b, q: (0, b * nq + q)),
            pl.BlockSpec((seq, LANES), lambda b, q: (b, 0)),
            pl.BlockSpec((seq, kv_lora), lambda b, q: (b, 0)),
            pl.BlockSpec((kv_lora, seq), lambda b, q: (0, b)),
            const2(wuqt), const3(wukt), const2(wqit), const3(wuvt),
        ],
        out_specs=pl.BlockSpec((tq, n_heads * vd), lambda b, q: (b * nq + q, 0)),
        out_shape=jax.ShapeDtypeStruct((n, n_heads * vd), BF16),
        scratch_shapes=[
            pltpu.VMEM((n_heads, kv_lora, tq), BF16),
            pltpu.VMEM((n_heads, LANES, tq), BF16),
            pltpu.VMEM((n_heads, SUBLANES, tq), F32),
            pltpu.VMEM((seq, tq), jnp.int32),
            pltpu.VMEM((seq, tq), F32),
            pltpu.VMEM((n_heads, kv_lora, tq), BF16),
            pltpu.VMEM((2, seq, tq), F32),
            pltpu.VMEM((2, SUBLANES, tq), F32),
            pltpu.VMEM((2, SUBLANES, tq), F32),
            pltpu.VMEM((2, seq, tq), BF16),
            pltpu.VMEM((SUBLANES, tq), jnp.int32),
            pltpu.SMEM((1,), jnp.int32),
        ],
        compiler_params=_params(2),
        name="dsa_attention",
    )(cqt, widxt, kidx, ckv, ckvt, wuqt, wukt, wqit, wuvt)


def _out_proj_kernel(h_ref, o_ref, w_ref, out_ref):
    out_ref[...] = h_ref[...] + _dot(o_ref[...], w_ref[...])


def _out_proj(h, o, w, *, tm):
    n, d = h.shape
    k = o.shape[1]
    return pl.pallas_call(
        _out_proj_kernel,
        grid=(n // tm,),
        in_specs=[
            pl.BlockSpec((tm, d), lambda i: (i, 0)),
            pl.BlockSpec((tm, k), lambda i: (i, 0)),
            pl.BlockSpec((k, d), lambda i: (0, 0)),
        ],
        out_specs=pl.BlockSpec((tm, d), lambda i: (i, 0)),
        out_shape=jax.ShapeDtypeStruct((n, d), F32),
        compiler_params=_params(1),
        name="attn_out_proj",
    )(h, o, w)


def _pad_last(a, width):
    return jnp.pad(a, [(0, 0)] * (a.ndim - 1) + [(0, width - a.shape[-1])])


def _tile(n, want):
    t = min(n, want)
    assert n % t == 0, (n, t)
    return t


def kernel(x, norm_mix, norm_mlp, mlp_w1, mlp_w2, conv_in, conv_w, conv_out, attn_in,
           q_norm, kv_norm, w_uq, w_uk, w_uv, w_qidx, kidx_ln_g, kidx_ln_b, attn_out,
           final_norm):
    batch, seq, d = x.shape
    depth = norm_mix.shape[0]
    q_lora, n_heads, hd = w_uq.shape[1:]
    kv_lora = w_uk.shape[3]
    idx_heads, idx_dim = w_qidx.shape[2:]
    assert idx_heads == n_heads and idx_dim <= LANES and n_heads <= LANES
    assert n_heads % SCORE_GROUP == 0 and n_heads % 2 == 0 and n_heads >= 4
    topk = min(TOPK_MAX, seq // 4)
    n = batch * seq

    tm = _tile(seq, 512)
    tf = _tile(mlp_w1.shape[2], 2048)
    tc = _tile(d, 512)
    tq = _tile(seq, 256)

    mlp_w1, mlp_w2, conv_in, conv_out, attn_out = (
        w.astype(BF16) for w in (mlp_w1, mlp_w2, conv_in, conv_out, attn_out))

    h = x.reshape(n, d)
    row = lambda v: v.reshape(1, -1)
    gf = row(final_norm)
    for i in range(depth):
        j = i // 2
        g = row(norm_mix[i])
        if i % 2 == 0:
            cw = jnp.pad(conv_w[j], ((0, 8 - conv_w.shape[1]), (0, 0)))
            h = _conv(h, g, conv_in[j], cw, conv_out[j], seq=seq, tm=tm, tc=tc)
        else:
            o = q_lora + kv_lora
            w = jnp.concatenate([
                attn_in[j][:, :o],
                _pad_last(attn_in[j][:, o:o + idx_dim], LANES),
                _pad_last(attn_in[j][:, o + idx_dim:], LANES)], axis=1).astype(BF16)
            cqt, ckv, ckvt, kidx, widxt = _attn_proj(
                h, g, w, row(q_norm[j]), row(kv_norm[j]),
                _pad_last(row(kidx_ln_g[j]), LANES), _pad_last(row(kidx_ln_b[j]), LANES),
                q_lora=q_lora, kv_lora=kv_lora, idx_dim=idx_dim,
                w_scale=float(idx_heads ** -0.5 * idx_dim ** -0.5), tm=tm)
            wuqt = w_uq[j].reshape(q_lora, n_heads * hd).T.astype(BF16)
            wukt = w_uk[j].transpose(0, 2, 1).astype(BF16)
            wqit = _pad_last(w_qidx[j], LANES).reshape(q_lora, n_heads * LANES).T.astype(BF16)
            wuvt = w_uv[j].transpose(0, 2, 1).astype(BF16)
            oh = _attn(cqt, widxt, kidx, ckv, ckvt, wuqt, wukt, wqit, wuvt,
                       batch=batch, seq=seq, topk=topk, scale=float(hd ** -0.5), tq=tq)
            h = _out_proj(h, oh, attn_out[j], tm=tm)
        h = _mlp(h, row(norm_mlp[i]), mlp_w1[i], mlp_w2[i], gf,
                 final=(i == depth - 1), tm=tm, tf=tf)
    return h.reshape(batch, seq, d)
```

```python
import functools

import jax
import jax.numpy as jnp
from jax import lax
from jax.experimental import pallas as pl
from jax.experimental.pallas import tpu as pltpu

EPS = 1e-6
TOPK_MAX = 256
LANES = 128
SUBLANES = 8
VMEM_LIMIT_BYTES = 56 * 1024 * 1024
INT_MIN = -2 ** 31
LOG2_E = 1.4426950408889634
SCORE_GROUP = 4
REDUCE_SLAB = 64
PIPE_UNROLL = 2

F32 = jnp.float32
BF16 = jnp.bfloat16


def _params(n_axes):
    return pltpu.CompilerParams(
        dimension_semantics=("arbitrary",) * n_axes,
        vmem_limit_bytes=VMEM_LIMIT_BYTES)


def _rms(x, g):
    return x * lax.rsqrt(jnp.mean(x * x, axis=-1, keepdims=True) + EPS) * g


def _dot(a, b):
    return jnp.dot(a, b, preferred_element_type=F32)


def _reduce_rows(op, x):
    rows, cols = x.shape
    if rows % REDUCE_SLAB == 0 and rows > REDUCE_SLAB:
        x = op(x.reshape(rows // REDUCE_SLAB, REDUCE_SLAB, cols), axis=0)
    return op(x, axis=0, keepdims=True)


def _mlp_kernel(h_ref, g_ref, w1_ref, w2_ref, gf_ref, o_ref, hn_ref, *, final):
    j = pl.program_id(1)

    @pl.when(j == 0)
    def _():
        h = h_ref[...]
        hn_ref[...] = _rms(h, g_ref[...]).astype(BF16)
        o_ref[...] = h

    a = jnp.maximum(_dot(hn_ref[...], w1_ref[...]), 0.0)
    o_ref[...] += _dot((a * a).astype(BF16), w2_ref[...])

    if final:
        @pl.when(j == pl.num_programs(1) - 1)
        def _():
            o_ref[...] = _rms(o_ref[...], gf_ref[...])


def _mlp(h, g, w1, w2, gf, *, layer, final, tm, tf):
    n, d = h.shape
    ff = w1.shape[2]
    return pl.pallas_call(
        functools.partial(_mlp_kernel, final=final),
        grid=(n // tm, ff // tf),
        in_specs=[
            pl.BlockSpec((tm, d), lambda i, j: (i, 0)),
            pl.BlockSpec((1, d), lambda i, j: (0, 0)),
            pl.BlockSpec((None, d, tf), lambda i, j: (layer, 0, j)),
            pl.BlockSpec((None, tf, d), lambda i, j: (layer, j, 0)),
            pl.BlockSpec((1, d), lambda i, j: (0, 0)),
        ],
        out_specs=pl.BlockSpec((tm, d), lambda i, j: (i, 0)),
        out_shape=jax.ShapeDtypeStruct((n, d), F32),
        scratch_shapes=[pltpu.VMEM((tm, d), BF16)],
        compiler_params=_params(2),
        name="sqrelu_mlp",
    )(h, g, w1, w2, gf)


def _conv_kernel(h_ref, g_ref, wb_ref, wc_ref, wx_ref, cw_ref, wo_ref, o_ref,
                 hn_ref, halo_ref, *, tiles_per_seq):
    i = pl.program_id(0)
    j = pl.program_id(1)

    @pl.when(j == 0)
    def _():
        h = h_ref[...]
        hn_ref[...] = _rms(h, g_ref[...]).astype(BF16)
        o_ref[...] = h

    hn = hn_ref[...]
    bg = _dot(hn, wb_ref[...])
    z = _dot(hn, wc_ref[...]) * _dot(hn, wx_ref[...])
    tm = z.shape[0]

    @pl.when(i % tiles_per_seq == 0)
    def _():
        halo_ref[j] = jnp.zeros(halo_ref.shape[1:], F32)

    prev = halo_ref[j]
    halo_ref[j] = z[tm - 8:, :]
    p1 = prev[7:8, :]
    p2 = prev[6:7, :]

    row = lax.broadcasted_iota(jnp.int32, (tm, 1), 0)
    z1 = jnp.where(row == 0, p1, pltpu.roll(z, 1, 0))
    z2 = jnp.where(row == 0, p2, jnp.where(row == 1, p1, pltpu.roll(z, 2, 0)))
    cw = cw_ref[...]
    zc = cw[0:1, :] * z2 + cw[1:2, :] * z1 + cw[2:3, :] * z
    o_ref[...] += _dot((bg * zc).astype(BF16), wo_ref[...])


def _conv(h, g, w_in, cw, w_out, *, layer, seq, tm, tc):
    n, d = h.shape
    nc = d // tc
    return pl.pallas_call(
        functools.partial(_conv_kernel, tiles_per_seq=seq // tm),
        grid=(n // tm, nc),
        in_specs=[
            pl.BlockSpec((tm, d), lambda i, j: (i, 0)),
            pl.BlockSpec((1, d), lambda i, j: (0, 0)),
            pl.BlockSpec((None, d, tc), lambda i, j: (layer, 0, j)),
            pl.BlockSpec((None, d, tc), lambda i, j: (layer, 0, j + nc)),
            pl.BlockSpec((None, d, tc), lambda i, j: (layer, 0, j + 2 * nc)),
            pl.BlockSpec((8, tc), lambda i, j: (0, j)),
            pl.BlockSpec((None, tc, d), lambda i, j: (layer, j, 0)),
        ],
        out_specs=pl.BlockSpec((tm, d), lambda i, j: (i, 0)),
        out_shape=jax.ShapeDtypeStruct((n, d), F32),
        scratch_shapes=[pltpu.VMEM((tm, d), BF16), pltpu.VMEM((nc, 8, tc), F32)],
        compiler_params=_params(2),
        name="short_conv",
    )(h, g, w_in, w_in, w_in, cw, w_out)


def _attn_proj_kernel(h_ref, g_ref, w_ref, qg_ref, kvg_ref, lng_ref, lnb_ref,
                      cqt_ref, ckv_ref, ckvt_ref, kidx_ref, widxt_ref,
                      *, q_lora, kv_lora, idx_dim, w_scale):
    hn = _rms(h_ref[...], g_ref[...]).astype(BF16)
    proj = _dot(hn, w_ref[...])
    cqt_ref[...] = _rms(proj[:, :q_lora], qg_ref[...]).T.astype(BF16)
    ckv = _rms(proj[:, q_lora:q_lora + kv_lora], kvg_ref[...])
    ckv_ref[...] = ckv.astype(BF16)
    ckvt_ref[...] = ckv.T.astype(BF16)

    o = q_lora + kv_lora
    kx = proj[:, o:o + LANES]
    valid = lax.broadcasted_iota(jnp.int32, (1, LANES), 1) < idx_dim
    mu = jnp.sum(kx, axis=-1, keepdims=True) * (1.0 / idx_dim)
    dlt = jnp.where(valid, kx - mu, 0.0)
    var = jnp.sum(dlt * dlt, axis=-1, keepdims=True) * (1.0 / idx_dim)
    kidx_ref[...] = (dlt * lax.rsqrt(var + EPS) * lng_ref[...] + lnb_ref[...]).astype(BF16)
    widxt_ref[...] = (proj[:, o + LANES:o + 2 * LANES] * w_scale).T


def _attn_proj(h, g, w, qg, kvg, lng, lnb, *, q_lora, kv_lora, idx_dim, w_scale, tm):
    n, d = h.shape
    cols = w.shape[1]
    row = lambda c: pl.BlockSpec((1, c), lambda i: (0, 0))
    return pl.pallas_call(
        functools.partial(_attn_proj_kernel, q_lora=q_lora, kv_lora=kv_lora,
                          idx_dim=idx_dim, w_scale=w_scale),
        grid=(n // tm,),
        in_specs=[
            pl.BlockSpec((tm, d), lambda i: (i, 0)),
            row(d),
            pl.BlockSpec((d, cols), lambda i: (0, 0)),
            row(q_lora), row(kv_lora), row(LANES), row(LANES),
        ],
        out_specs=[
            pl.BlockSpec((q_lora, tm), lambda i: (0, i)),
            pl.BlockSpec((tm, kv_lora), lambda i: (i, 0)),
            pl.BlockSpec((kv_lora, tm), lambda i: (0, i)),
            pl.BlockSpec((tm, LANES), lambda i: (i, 0)),
            pl.BlockSpec((LANES, tm), lambda i: (0, i)),
        ],
        out_shape=[
            jax.ShapeDtypeStruct((q_lora, n), BF16),
            jax.ShapeDtypeStruct((n, kv_lora), BF16),
            jax.ShapeDtypeStruct((kv_lora, n), BF16),
            jax.ShapeDtypeStruct((n, LANES), BF16),
            jax.ShapeDtypeStruct((LANES, n), F32),
        ],
        compiler_params=_params(1),
        name="attn_proj",
    )(h, g, w, qg, kvg, lng, lnb)


def _count(pred):
    return _reduce_rows(jnp.sum, pred.astype(F32))


def _attn_select(c, kidx_ref, qi_s, w_s, key_s, bias_s, thr_s, tie_s, *, n_heads, topk):
    seq, tq = key_s.shape
    s = (c + 1) * tq
    kpos = lax.broadcasted_iota(jnp.int32, (s, 1), 0)
    qpos = c * tq + lax.broadcasted_iota(jnp.int32, (1, tq), 1)
    causal = kpos <= qpos

    if s <= topk:
        bias_s[:s, :] = jnp.where(causal, 0.0, -jnp.inf)
        tie_s[0] = 0
        return

    def head_score(h):
        return w_s[h][0:1, :] * jnp.maximum(_dot(kidx_ref[:s, :], qi_s[h]), 0.0)

    def score_body(g, carry):
        h = g * SCORE_GROUP
        part = head_score(h)
        for d in range(1, SCORE_GROUP):
            part = part + head_score(h + d)
        bias_s[:s, :] += part
        return carry
    bias_s[:s, :] = jnp.zeros((s, tq), F32)
    lax.fori_loop(0, n_heads // SCORE_GROUP, score_body, 0)

    bits = pltpu.bitcast(bias_s[:s, :], jnp.int32)
    key = bits ^ ((bits >> 31) & 0x7FFFFFFF)
    key_s[:s, :] = jnp.where(causal, key, INT_MIN)
    if s < seq:
        key_s[s:, :] = jnp.full((seq - s, tq), INT_MIN, jnp.int32)

    def select_body(i, thr_u):
        cand_u = thr_u | jnp.left_shift(jnp.int32(1), 31 - i)
        ok = _count(key_s[:s, :] >= (cand_u ^ INT_MIN)) >= topk
        return jnp.where(ok, cand_u, thr_u)
    thr_u = lax.fori_loop(0, 32, select_body, jnp.zeros((1, tq), jnp.int32))
    thr = thr_u ^ INT_MIN
    thr_s[...] = jnp.broadcast_to(thr, thr_s.shape)

    ge = key_s[:s, :] >= thr
    if c * tq < topk:
        ge = ge & causal
    bias_s[:s, :] = jnp.where(ge, 0.0, -jnp.inf)
    tie_s[0] = (jnp.max(_count(ge)) > topk).astype(jnp.int32)


def _attn_ties(t0, key_s, bias_s, thr_s, *, topk):
    seq, tq = key_s.shape
    kpos = lax.broadcasted_iota(jnp.int32, (seq, 1), 0)
    qpos = t0 + lax.broadcasted_iota(jnp.int32, (1, tq), 1)
    causal = kpos <= qpos
    key = key_s[...]
    thr = thr_s[0:1, :]
    gt = (key > thr) & causal
    eq = (key == thr) & causal
    keep = topk - _count(gt)
    nbits = seq.bit_length()

    def cut_body(i, cut):
        cand = cut | jnp.left_shift(jnp.int32(1), nbits - 1 - i)
        ok = _count(eq & (kpos < cand)) <= keep
        return jnp.where(ok, cand, cut)
    cut = lax.fori_loop(0, nbits, cut_body, jnp.zeros((1, tq), jnp.int32))
    bias_s[...] = jnp.where(gt | (eq & (kpos < cut)), 0.0, -jnp.inf)


def _attn_heads(c, ckv_ref, ckvt_ref, ql_s, bias_s, ol_s, lg_s, m_s, d_s, e_s, *, n_heads):
    tq = bias_s.shape[1]
    s = (c + 1) * tq

    def logits(q, b):
        lg = _dot(ckv_ref[:s, :], q) + bias_s[:s, :]
        lg_s[b, :s, :] = lg
        m_s[b] = jnp.broadcast_to(_reduce_rows(jnp.max, lg), (SUBLANES, tq))

    def softmax(b):
        e = jnp.exp2(lg_s[b, :s, :] - m_s[b][0:1, :])
        d_s[b] = jnp.broadcast_to(_reduce_rows(jnp.sum, e), (SUBLANES, tq))
        e_s[b, :s, :] = e.astype(BF16)

    def pv(b):
        return (_dot(ckvt_ref[:, :s], e_s[b, :s, :]) / d_s[b][0:1, :]).astype(BF16)

    def steps(t0, count):
        qs = [ql_s[t0 + u] for u in range(count)]
        outs = []
        for u in range(count):
            outs.append(pv(u % 2))
            softmax((u + 1) % 2)
            logits(qs[u], u % 2)
        for u in range(count):
            ol_s[t0 + u - 2] = outs[u]

    logits(ql_s[0], 0)
    softmax(0)
    logits(ql_s[1], 1)

    n_steady = n_heads - 2
    peeled = n_steady % PIPE_UNROLL
    if peeled:
        steps(2, peeled)
    if n_steady // PIPE_UNROLL > 1:
        def pipe_body(p, carry):
            steps(2 + peeled + PIPE_UNROLL * p, PIPE_UNROLL)
            return carry
        lax.fori_loop(0, n_steady // PIPE_UNROLL, pipe_body, 0)
    elif n_steady // PIPE_UNROLL == 1:
        steps(2 + peeled, PIPE_UNROLL)

    oa = pv(0)
    softmax(1)
    ol_s[n_heads - 2] = oa
    ol_s[n_heads - 1] = pv(1)


def _attn_kernel(cqt_ref, widxt_ref, kidx_ref, ckv_ref, ckvt_ref,
                 wuqt_ref, wukt_ref, wqit_ref, wuvt_ref, o_ref,
                 ql_s, qi_s, w_s, key_s, bias_s, ol_s, lg_s, m_s, d_s, e_s, thr_s, tie_s,
                 *, n_heads, topk, qscale):
    seq, tq = key_s.shape
    hd = wukt_ref.shape[2]
    vd = wuvt_ref.shape[1]
    cqt = cqt_ref[...]
    q_all = _dot(wuqt_ref[...], cqt).astype(BF16)
    qi_all = _dot(wqit_ref[...], cqt).astype(BF16)
    widxt = widxt_ref[...]
    for h in range(n_heads):
        q_lat = _dot(wukt_ref[h], q_all[h * hd:(h + 1) * hd, :]) * qscale
        ql_s[h] = q_lat.astype(BF16)
        qi_s[h] = qi_all[h * LANES:(h + 1) * LANES, :]
        w_s[h] = jnp.broadcast_to(widxt[h:h + 1, :], (SUBLANES, tq))

    qtile = pl.program_id(1)
    for c in range(seq // tq):
        pl.when(qtile == c)(functools.partial(
            _attn_select, c, kidx_ref, qi_s, w_s, key_s, bias_s, thr_s, tie_s,
            n_heads=n_heads, topk=topk))
    pl.when(tie_s[0] != 0)(functools.partial(
        _attn_ties, qtile * tq, key_s, bias_s, thr_s, topk=topk))
    for c in range(seq // tq):
        pl.when(qtile == c)(functools.partial(
            _attn_heads, c, ckv_ref, ckvt_ref, ql_s, bias_s, ol_s, lg_s, m_s, d_s, e_s,
            n_heads=n_heads))

    for h in range(n_heads):
        o_ref[:, h * vd:(h + 1) * vd] = _dot(wuvt_ref[h], ol_s[h]).T.astype(BF16)


def _attn(cqt, widxt, kidx, ckv, ckvt, wuqt, wukt, wqit, wuvt, *, batch, seq, topk, scale, tq):
    q_lora, n = cqt.shape
    n_heads, kv_lora, hd = wukt.shape
    vd = wuvt.shape[1]
    nq = seq // tq
    const2 = lambda a: pl.BlockSpec(a.shape, lambda b, q: (0, 0))
    const3 = lambda a: pl.BlockSpec(a.shape, lambda b, q: (0, 0, 0))
    return pl.pallas_call(
        functools.partial(_attn_kernel, n_heads=n_heads, topk=topk,
                          qscale=scale * LOG2_E),
        grid=(batch, nq),
        in_specs=[
            pl.BlockSpec((q_lora, tq), lambda b, q: (0, b * nq + q)),
            pl.BlockSpec((LANES, tq), lambda b, q: (0, b * nq + q)),
            pl.BlockSpec((seq, LANES), lambda b, q: (b, 0)),
            pl.BlockSpec((seq, kv_lora), lambda b, q: (b, 0)),
            pl.BlockSpec((kv_lora, seq), lambda b, q: (0, b)),
            const2(wuqt), const3(wukt), const2(wqit), const3(wuvt),
        ],
        out_specs=pl.BlockSpec((tq, n_heads * vd), lambda b, q: (b * nq + q, 0)),
        out_shape=jax.ShapeDtypeStruct((n, n_heads * vd), BF16),
        scratch_shapes=[
            pltpu.VMEM((n_heads, kv_lora, tq), BF16),
            pltpu.VMEM((n_heads, LANES, tq), BF16),
            pltpu.VMEM((n_heads, SUBLANES, tq), F32),
            pltpu.VMEM((seq, tq), jnp.int32),
            pltpu.VMEM((seq, tq), F32),
            pltpu.VMEM((n_heads, kv_lora, tq), BF16),
            pltpu.VMEM((2, seq, tq), F32),
            pltpu.VMEM((2, SUBLANES, tq), F32),
            pltpu.VMEM((2, SUBLANES, tq), F32),
            pltpu.VMEM((2, seq, tq), BF16),
            pltpu.VMEM((SUBLANES, tq), jnp.int32),
            pltpu.SMEM((1,), jnp.int32),
        ],
        compiler_params=_params(2),
        name="dsa_attention",
    )(cqt, widxt, kidx, ckv, ckvt, wuqt, wukt, wqit, wuvt)


def _out_proj_kernel(h_ref, o_ref, w_ref, out_ref):
    out_ref[...] = h_ref[...] + _dot(o_ref[...], w_ref[...])


def _out_proj(h, o, w, *, layer, tm):
    n, d = h.shape
    k = o.shape[1]
    return pl.pallas_call(
        _out_proj_kernel,
        grid=(n // tm,),
        in_specs=[
            pl.BlockSpec((tm, d), lambda i: (i, 0)),
            pl.BlockSpec((tm, k), lambda i: (i, 0)),
            pl.BlockSpec((None, k, d), lambda i: (layer, 0, 0)),
        ],
        out_specs=pl.BlockSpec((tm, d), lambda i: (i, 0)),
        out_shape=jax.ShapeDtypeStruct((n, d), F32),
        compiler_params=_params(1),
        name="attn_out_proj",
    )(h, o, w)


def _pad_last(a, width):
    return jnp.pad(a, [(0, 0)] * (a.ndim - 1) + [(0, width - a.shape[-1])])


def _tile(n, want):
    t = min(n, want)
    assert n % t == 0, (n, t)
    return t


def kernel(x, norm_mix, norm_mlp, mlp_w1, mlp_w2, conv_in, conv_w, conv_out, attn_in,
           q_norm, kv_norm, w_uq, w_uk, w_uv, w_qidx, kidx_ln_g, kidx_ln_b, attn_out,
           final_norm):
    batch, seq, d = x.shape
    depth = norm_mix.shape[0]
    q_lora, n_heads, hd = w_uq.shape[1:]
    kv_lora = w_uk.shape[3]
    idx_heads, idx_dim = w_qidx.shape[2:]
    assert idx_heads == n_heads and idx_dim <= LANES and n_heads <= LANES
    assert n_heads % SCORE_GROUP == 0 and n_heads % 2 == 0 and n_heads >= 4
    topk = min(TOPK_MAX, seq // 4)
    n = batch * seq

    tm = _tile(seq, 512)
    tf = _tile(mlp_w1.shape[2], 2048)
    tc = _tile(d, 512)
    tq = _tile(seq, 256)

    mlp_w1, mlp_w2, conv_in, conv_out, attn_out = (
        w.astype(BF16) for w in (mlp_w1, mlp_w2, conv_in, conv_out, attn_out))

    h = x.reshape(n, d)
    row = lambda v: v.reshape(1, -1)
    gf = row(final_norm)
    for i in range(depth):
        j = i // 2
        g = row(norm_mix[i])
        if i % 2 == 0:
            cw = jnp.pad(conv_w[j], ((0, 8 - conv_w.shape[1]), (0, 0)))
            h = _conv(h, g, conv_in, cw, conv_out, layer=j, seq=seq, tm=tm, tc=tc)
        else:
            o = q_lora + kv_lora
            w = jnp.concatenate([
                attn_in[j][:, :o],
                _pad_last(attn_in[j][:, o:o + idx_dim], LANES),
                _pad_last(attn_in[j][:, o + idx_dim:], LANES)], axis=1).astype(BF16)
            cqt, ckv, ckvt, kidx, widxt = _attn_proj(
                h, g, w, row(q_norm[j]), row(kv_norm[j]),
                _pad_last(row(kidx_ln_g[j]), LANES), _pad_last(row(kidx_ln_b[j]), LANES),
                q_lora=q_lora, kv_lora=kv_lora, idx_dim=idx_dim,
                w_scale=float(idx_heads ** -0.5 * idx_dim ** -0.5), tm=tm)
            wuqt = w_uq[j].reshape(q_lora, n_heads * hd).T.astype(BF16)
            wukt = w_uk[j].transpose(0, 2, 1).astype(BF16)
            wqit = _pad_last(w_qidx[j], LANES).reshape(q_lora, n_heads * LANES).T.astype(BF16)
            wuvt = w_uv[j].transpose(0, 2, 1).astype(BF16)
            oh = _attn(cqt, widxt, kidx, ckv, ckvt, wuqt, wukt, wqit, wuvt,
                       batch=batch, seq=seq, topk=topk, scale=float(hd ** -0.5), tq=tq)
            h = _out_proj(h, oh, attn_out, layer=j, tm=tm)
        h = _mlp(h, row(norm_mlp[i]), mlp_w1, mlp_w2, gf,
                 layer=i, final=(i == depth - 1), tm=tm, tf=tf)
    return h.reshape(batch, seq, d)
```

```python
import functools

import jax
import jax.numpy as jnp
from jax import lax
from jax.experimental import pallas as pl
from jax.experimental.pallas import tpu as pltpu

EPS = 1e-6
TOPK_MAX = 256
LANES = 128
SUBLANES = 8
VMEM_LIMIT_BYTES = 56 * 1024 * 1024
INT_MIN = -2 ** 31
LOG2_E = 1.4426950408889634
SCORE_GROUP = 4
REDUCE_SLAB = 64
PIPE_UNROLL = 2

F32 = jnp.float32
BF16 = jnp.bfloat16


def _params(n_axes):
    return pltpu.CompilerParams(
        dimension_semantics=("arbitrary",) * n_axes,
        vmem_limit_bytes=VMEM_LIMIT_BYTES)


def _rms(x, g):
    return x * lax.rsqrt(jnp.mean(x * x, axis=-1, keepdims=True) + EPS) * g


def _dot(a, b):
    return jnp.dot(a, b, preferred_element_type=F32)


def _reduce_rows(op, x):
    rows, cols = x.shape
    if rows % REDUCE_SLAB == 0 and rows > REDUCE_SLAB:
        x = op(x.reshape(rows // REDUCE_SLAB, REDUCE_SLAB, cols), axis=0)
    return op(x, axis=0, keepdims=True)


def _mlp_kernel(h_ref, g_ref, w1_ref, w2_ref, gf_ref, o_ref, hn_ref, *, final):
    j = pl.program_id(1)

    def update(first):
        if first:
            hn_ref[...] = _rms(h_ref[...], g_ref[...]).astype(BF16)
        a = jnp.maximum(_dot(hn_ref[...], w1_ref[...]), 0.0)
        r = _dot((a * a).astype(BF16), w2_ref[...])
        if first:
            o_ref[...] = h_ref[...] + r
        else:
            o_ref[...] += r

    pl.when(j == 0)(functools.partial(update, True))
    pl.when(j > 0)(functools.partial(update, False))

    if final:
        @pl.when(j == pl.num_programs(1) - 1)
        def _():
            o_ref[...] = _rms(o_ref[...], gf_ref[...])


def _mlp(h, g, w1, w2, gf, *, layer, final, tm, tf):
    n, d = h.shape
    ff = w1.shape[2]
    return pl.pallas_call(
        functools.partial(_mlp_kernel, final=final),
        grid=(n // tm, ff // tf),
        in_specs=[
            pl.BlockSpec((tm, d), lambda i, j: (i, 0)),
            pl.BlockSpec((1, d), lambda i, j: (0, 0)),
            pl.BlockSpec((None, d, tf), lambda i, j: (layer, 0, j)),
            pl.BlockSpec((None, tf, d), lambda i, j: (layer, j, 0)),
            pl.BlockSpec((1, d), lambda i, j: (0, 0)),
        ],
        out_specs=pl.BlockSpec((tm, d), lambda i, j: (i, 0)),
        out_shape=jax.ShapeDtypeStruct((n, d), F32),
        scratch_shapes=[pltpu.VMEM((tm, d), BF16)],
        compiler_params=_params(2),
        name="sqrelu_mlp",
    )(h, g, w1, w2, gf)


def _conv_kernel(h_ref, g_ref, wb_ref, wc_ref, wx_ref, cw_ref, wo_ref, o_ref,
                 hn_ref, halo_ref, *, tiles_per_seq):
    i = pl.program_id(0)
    j = pl.program_id(1)

    @pl.when(j == 0)
    def _():
        h = h_ref[...]
        hn_ref[...] = _rms(h, g_ref[...]).astype(BF16)
        o_ref[...] = h

    hn = hn_ref[...]
    bg = _dot(hn, wb_ref[...])
    z = _dot(hn, wc_ref[...]) * _dot(hn, wx_ref[...])
    tm = z.shape[0]

    @pl.when(i % tiles_per_seq == 0)
    def _():
        halo_ref[j] = jnp.zeros(halo_ref.shape[1:], F32)

    prev = halo_ref[j]
    halo_ref[j] = z[tm - 8:, :]
    p1 = prev[7:8, :]
    p2 = prev[6:7, :]

    row = lax.broadcasted_iota(jnp.int32, (tm, 1), 0)
    z1 = jnp.where(row == 0, p1, pltpu.roll(z, 1, 0))
    z2 = jnp.where(row == 0, p2, jnp.where(row == 1, p1, pltpu.roll(z, 2, 0)))
    cw = cw_ref[...]
    zc = cw[0:1, :] * z2 + cw[1:2, :] * z1 + cw[2:3, :] * z
    o_ref[...] += _dot((bg * zc).astype(BF16), wo_ref[...])


def _conv(h, g, w_in, cw, w_out, *, layer, seq, tm, tc):
    n, d = h.shape
    nc = d // tc
    return pl.pallas_call(
        functools.partial(_conv_kernel, tiles_per_seq=seq // tm),
        grid=(n // tm, nc),
        in_specs=[
            pl.BlockSpec((tm, d), lambda i, j: (i, 0)),
            pl.BlockSpec((1, d), lambda i, j: (0, 0)),
            pl.BlockSpec((None, d, tc), lambda i, j: (layer, 0, j)),
            pl.BlockSpec((None, d, tc), lambda i, j: (layer, 0, j + nc)),
            pl.BlockSpec((None, d, tc), lambda i, j: (layer, 0, j + 2 * nc)),
            pl.BlockSpec((8, tc), lambda i, j: (0, j)),
            pl.BlockSpec((None, tc, d), lambda i, j: (layer, j, 0)),
        ],
        out_specs=pl.BlockSpec((tm, d), lambda i, j: (i, 0)),
        out_shape=jax.ShapeDtypeStruct((n, d), F32),
        scratch_shapes=[pltpu.VMEM((tm, d), BF16), pltpu.VMEM((nc, 8, tc), F32)],
        compiler_params=_params(2),
        name="short_conv",
    )(h, g, w_in, w_in, w_in, cw, w_out)


def _attn_proj_kernel(h_ref, g_ref, w_ref, qg_ref, kvg_ref, lng_ref, lnb_ref,
                      cqt_ref, ckv_ref, ckvt_ref, kidx_ref, widxt_ref,
                      *, q_lora, kv_lora, idx_dim, w_scale):
    hn = _rms(h_ref[...], g_ref[...]).astype(BF16)
    proj = _dot(hn, w_ref[...])
    cqt_ref[...] = _rms(proj[:, :q_lora], qg_ref[...]).T.astype(BF16)
    ckv = _rms(proj[:, q_lora:q_lora + kv_lora], kvg_ref[...])
    ckv_ref[...] = ckv.astype(BF16)
    ckvt_ref[...] = ckv.T.astype(BF16)

    o = q_lora + kv_lora
    kx = proj[:, o:o + LANES]
    valid = lax.broadcasted_iota(jnp.int32, (1, LANES), 1) < idx_dim
    mu = jnp.sum(kx, axis=-1, keepdims=True) * (1.0 / idx_dim)
    dlt = jnp.where(valid, kx - mu, 0.0)
    var = jnp.sum(dlt * dlt, axis=-1, keepdims=True) * (1.0 / idx_dim)
    kidx_ref[...] = (dlt * lax.rsqrt(var + EPS) * lng_ref[...] + lnb_ref[...]).astype(BF16)
    widxt_ref[...] = (proj[:, o + LANES:o + 2 * LANES] * w_scale).T


def _attn_proj(h, g, w, qg, kvg, lng, lnb, *, q_lora, kv_lora, idx_dim, w_scale, tm):
    n, d = h.shape
    cols = w.shape[1]
    row = lambda c: pl.BlockSpec((1, c), lambda i: (0, 0))
    return pl.pallas_call(
        functools.partial(_attn_proj_kernel, q_lora=q_lora, kv_lora=kv_lora,
                          idx_dim=idx_dim, w_scale=w_scale),
        grid=(n // tm,),
        in_specs=[
            pl.BlockSpec((tm, d), lambda i: (i, 0)),
            row(d),
            pl.BlockSpec((d, cols), lambda i: (0, 0)),
            row(q_lora), row(kv_lora), row(LANES), row(LANES),
        ],
        out_specs=[
            pl.BlockSpec((q_lora, tm), lambda i: (0, i)),
            pl.BlockSpec((tm, kv_lora), lambda i: (i, 0)),
            pl.BlockSpec((kv_lora, tm), lambda i: (0, i)),
            pl.BlockSpec((tm, LANES), lambda i: (i, 0)),
            pl.BlockSpec((LANES, tm), lambda i: (0, i)),
        ],
        out_shape=[
            jax.ShapeDtypeStruct((q_lora, n), BF16),
            jax.ShapeDtypeStruct((n, kv_lora), BF16),
            jax.ShapeDtypeStruct((kv_lora, n), BF16),
            jax.ShapeDtypeStruct((n, LANES), BF16),
            jax.ShapeDtypeStruct((LANES, n), F32),
        ],
        compiler_params=_params(1),
        name="attn_proj",
    )(h, g, w, qg, kvg, lng, lnb)


def _count(pred):
    return _reduce_rows(jnp.sum, pred.astype(F32))


def _attn_select(c, kidx_ref, qi_s, w_s, key_s, bias_s, thr_s, tie_s, *, n_heads, topk):
    seq, tq = key_s.shape
    s = (c + 1) * tq
    kpos = lax.broadcasted_iota(jnp.int32, (s, 1), 0)
    qpos = c * tq + lax.broadcasted_iota(jnp.int32, (1, tq), 1)
    causal = kpos <= qpos

    if s <= topk:
        bias_s[:s, :] = jnp.where(causal, 0.0, -jnp.inf)
        tie_s[0] = 0
        return

    def head_score(h):
        return w_s[h][0:1, :] * jnp.maximum(_dot(kidx_ref[:s, :], qi_s[h]), 0.0)

    def score_body(g, carry):
        h = g * SCORE_GROUP
        part = head_score(h)
        for d in range(1, SCORE_GROUP):
            part = part + head_score(h + d)
        bias_s[:s, :] += part
        return carry
    bias_s[:s, :] = jnp.zeros((s, tq), F32)
    lax.fori_loop(0, n_heads // SCORE_GROUP, score_body, 0)

    bits = pltpu.bitcast(bias_s[:s, :], jnp.int32)
    key = bits ^ ((bits >> 31) & 0x7FFFFFFF)
    key_s[:s, :] = jnp.where(causal, key, INT_MIN)
    if s < seq:
        key_s[s:, :] = jnp.full((seq - s, tq), INT_MIN, jnp.int32)

    def select_body(i, thr_u):
        cand_u = thr_u | jnp.left_shift(jnp.int32(1), 31 - i)
        ok = _count(key_s[:s, :] >= (cand_u ^ INT_MIN)) >= topk
        return jnp.where(ok, cand_u, thr_u)
    thr_u = lax.fori_loop(0, 32, select_body, jnp.zeros((1, tq), jnp.int32))
    thr = thr_u ^ INT_MIN
    thr_s[...] = jnp.broadcast_to(thr, thr_s.shape)

    ge = key_s[:s, :] >= thr
    if c * tq < topk:
        ge = ge & causal
    bias_s[:s, :] = jnp.where(ge, 0.0, -jnp.inf)
    tie_s[0] = (jnp.max(_count(ge)) > topk).astype(jnp.int32)


def _attn_ties(t0, key_s, bias_s, thr_s, *, topk):
    seq, tq = key_s.shape
    kpos = lax.broadcasted_iota(jnp.int32, (seq, 1), 0)
    qpos = t0 + lax.broadcasted_iota(jnp.int32, (1, tq), 1)
    causal = kpos <= qpos
    key = key_s[...]
    thr = thr_s[0:1, :]
    gt = (key > thr) & causal
    eq = (key == thr) & causal
    keep = topk - _count(gt)
    nbits = seq.bit_length()

    def cut_body(i, cut):
        cand = cut | jnp.left_shift(jnp.int32(1), nbits - 1 - i)
        ok = _count(eq & (kpos < cand)) <= keep
        return jnp.where(ok, cand, cut)
    cut = lax.fori_loop(0, nbits, cut_body, jnp.zeros((1, tq), jnp.int32))
    bias_s[...] = jnp.where(gt | (eq & (kpos < cut)), 0.0, -jnp.inf)


def _attn_heads(c, ckv_ref, ckvt_ref, ql_s, bias_s, ol_s, lg_s, m_s, d_s, e_s, *, n_heads):
    tq = bias_s.shape[1]
    s = (c + 1) * tq

    def logits(q, b):
        lg = _dot(ckv_ref[:s, :], q) + bias_s[:s, :]
        lg_s[b, :s, :] = lg
        m_s[b] = jnp.broadcast_to(_reduce_rows(jnp.max, lg), (SUBLANES, tq))

    def softmax(b):
        e = jnp.exp2(lg_s[b, :s, :] - m_s[b][0:1, :])
        d_s[b] = jnp.broadcast_to(_reduce_rows(jnp.sum, e), (SUBLANES, tq))
        e_s[b, :s, :] = e.astype(BF16)

    def pv(b):
        return (_dot(ckvt_ref[:, :s], e_s[b, :s, :]) / d_s[b][0:1, :]).astype(BF16)

    def steps(t0, count):
        qs = [ql_s[t0 + u] for u in range(count)]
        outs = []
        for u in range(count):
            outs.append(pv(u % 2))
            softmax((u + 1) % 2)
            logits(qs[u], u % 2)
        for u in range(count):
            ol_s[t0 + u - 2] = outs[u]

    logits(ql_s[0], 0)
    softmax(0)
    logits(ql_s[1], 1)

    n_steady = n_heads - 2
    peeled = n_steady % PIPE_UNROLL
    if peeled:
        steps(2, peeled)
    if n_steady // PIPE_UNROLL > 1:
        def pipe_body(p, carry):
            steps(2 + peeled + PIPE_UNROLL * p, PIPE_UNROLL)
            return carry
        lax.fori_loop(0, n_steady // PIPE_UNROLL, pipe_body, 0)
    elif n_steady // PIPE_UNROLL == 1:
        steps(2 + peeled, PIPE_UNROLL)

    oa = pv(0)
    softmax(1)
    ol_s[n_heads - 2] = oa
    ol_s[n_heads - 1] = pv(1)


def _attn_kernel(cqt_ref, widxt_ref, kidx_ref, ckv_ref, ckvt_ref,
                 wuqt_ref, wukt_ref, wqit_ref, wuvt_ref, o_ref,
                 ql_s, qi_s, w_s, key_s, bias_s, ol_s, lg_s, m_s, d_s, e_s, thr_s, tie_s,
                 *, n_heads, topk, qscale):
    seq, tq = key_s.shape
    hd = wukt_ref.shape[2]
    vd = wuvt_ref.shape[1]
    cqt = cqt_ref[...]
    q_all = _dot(wuqt_ref[...], cqt).astype(BF16)
    qi_all = _dot(wqit_ref[...], cqt).astype(BF16)
    widxt = widxt_ref[...]
    for h in range(n_heads):
        q_lat = _dot(wukt_ref[h], q_all[h * hd:(h + 1) * hd, :]) * qscale
        ql_s[h] = q_lat.astype(BF16)
        qi_s[h] = qi_all[h * LANES:(h + 1) * LANES, :]
        w_s[h] = jnp.broadcast_to(widxt[h:h + 1, :], (SUBLANES, tq))

    qtile = pl.program_id(1)
    for c in range(seq // tq):
        pl.when(qtile == c)(functools.partial(
            _attn_select, c, kidx_ref, qi_s, w_s, key_s, bias_s, thr_s, tie_s,
            n_heads=n_heads, topk=topk))
    pl.when(tie_s[0] != 0)(functools.partial(
        _attn_ties, qtile * tq, key_s, bias_s, thr_s, topk=topk))
    for c in range(seq // tq):
        pl.when(qtile == c)(functools.partial(
            _attn_heads, c, ckv_ref, ckvt_ref, ql_s, bias_s, ol_s, lg_s, m_s, d_s, e_s,
            n_heads=n_heads))

    for h in range(n_heads):
        o_ref[:, h * vd:(h + 1) * vd] = _dot(wuvt_ref[h], ol_s[h]).T.astype(BF16)


def _attn(cqt, widxt, kidx, ckv, ckvt, wuqt, wukt, wqit, wuvt, *, batch, seq, topk, scale, tq):
    q_lora, n = cqt.shape
    n_heads, kv_lora, hd = wukt.shape
    vd = wuvt.shape[1]
    nq = seq // tq
    const2 = lambda a: pl.BlockSpec(a.shape, lambda b, q: (0, 0))
    const3 = lambda a: pl.BlockSpec(a.shape, lambda b, q: (0, 0, 0))
    return pl.pallas_call(
        functools.partial(_attn_kernel, n_heads=n_heads, topk=topk,
                          qscale=scale * LOG2_E),
        grid=(batch, nq),
        in_specs=[
            pl.BlockSpec((q_lora, tq), lambda b, q: (0, b * nq + q)),
            pl.BlockSpec((LANES, tq), lambda b, q: (0, b * nq + q)),
            pl.BlockSpec((seq, LANES), lambda b, q: (b, 0)),
            pl.BlockSpec((seq, kv_lora), lambda b, q: (b, 0)),
            pl.BlockSpec((kv_lora, seq), lambda b, q: (0, b)),
            const2(wuqt), const3(wukt), const2(wqit), const3(wuvt),
        ],
        out_specs=pl.BlockSpec((tq, n_heads * vd), lambda b, q: (b * nq + q, 0)),
        out_shape=jax.ShapeDtypeStruct((n, n_heads * vd), BF16),
        scratch_shapes=[
            pltpu.VMEM((n_heads, kv_lora, tq), BF16),
            pltpu.VMEM((n_heads, LANES, tq), BF16),
            pltpu.VMEM((n_heads, SUBLANES, tq), F32),
            pltpu.VMEM((seq, tq), jnp.int32),
            pltpu.VMEM((seq, tq), F32),
            pltpu.VMEM((n_heads, kv_lora, tq), BF16),
            pltpu.VMEM((2, seq, tq), F32),
            pltpu.VMEM((2, SUBLANES, tq), F32),
            pltpu.VMEM((2, SUBLANES, tq), F32),
            pltpu.VMEM((2, seq, tq), BF16),
            pltpu.VMEM((SUBLANES, tq), jnp.int32),
            pltpu.SMEM((1,), jnp.int32),
        ],
        compiler_params=_params(2),
        name="dsa_attention",
    )(cqt, widxt, kidx, ckv, ckvt, wuqt, wukt, wqit, wuvt)


def _out_proj_kernel(h_ref, o_ref, w_ref, out_ref):
    out_ref[...] = h_ref[...] + _dot(o_ref[...], w_ref[...])


def _out_proj(h, o, w, *, layer, tm):
    n, d = h.shape
    k = o.shape[1]
    return pl.pallas_call(
        _out_proj_kernel,
        grid=(n // tm,),
        in_specs=[
            pl.BlockSpec((tm, d), lambda i: (i, 0)),
            pl.BlockSpec((tm, k), lambda i: (i, 0)),
            pl.BlockSpec((None, k, d), lambda i: (layer, 0, 0)),
        ],
        out_specs=pl.BlockSpec((tm, d), lambda i: (i, 0)),
        out_shape=jax.ShapeDtypeStruct((n, d), F32),
        compiler_params=_params(1),
        name="attn_out_proj",
    )(h, o, w)


def _pad_last(a, width):
    return jnp.pad(a, [(0, 0)] * (a.ndim - 1) + [(0, width - a.shape[-1])])


def _tile(n, want):
    t = min(n, want)
    assert n % t == 0, (n, t)
    return t


def kernel(x, norm_mix, norm_mlp, mlp_w1, mlp_w2, conv_in, conv_w, conv_out, attn_in,
           q_norm, kv_norm, w_uq, w_uk, w_uv, w_qidx, kidx_ln_g, kidx_ln_b, attn_out,
           final_norm):
    batch, seq, d = x.shape
    depth = norm_mix.shape[0]
    q_lora, n_heads, hd = w_uq.shape[1:]
    kv_lora = w_uk.shape[3]
    idx_heads, idx_dim = w_qidx.shape[2:]
    assert idx_heads == n_heads and idx_dim <= LANES and n_heads <= LANES
    assert n_heads % SCORE_GROUP == 0 and n_heads % 2 == 0 and n_heads >= 4
    topk = min(TOPK_MAX, seq // 4)
    n = batch * seq

    tm = _tile(seq, 512)
    tf = _tile(mlp_w1.shape[2], 2048)
    tc = _tile(d, 512)
    tq = _tile(seq, 256)

    mlp_w1, mlp_w2, conv_in, conv_out, attn_out = (
        w.astype(BF16) for w in (mlp_w1, mlp_w2, conv_in, conv_out, attn_out))

    h = x.reshape(n, d)
    row = lambda v: v.reshape(1, -1)
    gf = row(final_norm)
    for i in range(depth):
        j = i // 2
        g = row(norm_mix[i])
        if i % 2 == 0:
            cw = jnp.pad(conv_w[j], ((0, 8 - conv_w.shape[1]), (0, 0)))
            h = _conv(h, g, conv_in, cw, conv_out, layer=j, seq=seq, tm=tm, tc=tc)
        else:
            o = q_lora + kv_lora
            w = jnp.concatenate([
                attn_in[j][:, :o],
                _pad_last(attn_in[j][:, o:o + idx_dim], LANES),
                _pad_last(attn_in[j][:, o + idx_dim:], LANES)], axis=1).astype(BF16)
            cqt, ckv, ckvt, kidx, widxt = _attn_proj(
                h, g, w, row(q_norm[j]), row(kv_norm[j]),
                _pad_last(row(kidx_ln_g[j]), LANES), _pad_last(row(kidx_ln_b[j]), LANES),
                q_lora=q_lora, kv_lora=kv_lora, idx_dim=idx_dim,
                w_scale=float(idx_heads ** -0.5 * idx_dim ** -0.5), tm=tm)
            wuqt = w_uq[j].reshape(q_lora, n_heads * hd).T.astype(BF16)
            wukt = w_uk[j].transpose(0, 2, 1).astype(BF16)
            wqit = _pad_last(w_qidx[j], LANES).reshape(q_lora, n_heads * LANES).T.astype(BF16)
            wuvt = w_uv[j].transpose(0, 2, 1).astype(BF16)
            oh = _attn(cqt, widxt, kidx, ckv, ckvt, wuqt, wukt, wqit, wuvt,
                       batch=batch, seq=seq, topk=topk, scale=float(hd ** -0.5), tq=tq)
            h = _out_proj(h, oh, attn_out, layer=j, tm=tm)
        h = _mlp(h, row(norm_mlp[i]), mlp_w1, mlp_w2, gf,
                 layer=i, final=(i == depth - 1), tm=tm, tf=tf)
    return h.reshape(batch, seq, d)
```

```python
import functools

import jax
import jax.numpy as jnp
from jax import lax
from jax.experimental import pallas as pl
from jax.experimental.pallas import tpu as pltpu

EPS = 1e-6
TOPK_MAX = 256
LANES = 128
SUBLANES = 8
VMEM_LIMIT_BYTES = 56 * 1024 * 1024
INT_MIN = -2 ** 31
KEY_BITS = 32
CONV_WIDTH = 3
LOG2_E = 1.4426950408889634
SCORE_GROUP = 4
REDUCE_SLAB = 64
PIPE_UNROLL = 2

F32 = jnp.float32
BF16 = jnp.bfloat16


def _params(n_axes):
    return pltpu.CompilerParams(
        dimension_semantics=("arbitrary",) * n_axes,
        vmem_limit_bytes=VMEM_LIMIT_BYTES)


def _rms(x, g):
    return x * lax.rsqrt(jnp.mean(x * x, axis=-1, keepdims=True) + EPS) * g


def _dot(a, b):
    return jnp.dot(a, b, preferred_element_type=F32)


def _reduce_rows(op, x):
    rows, cols = x.shape
    if rows % REDUCE_SLAB == 0 and rows > REDUCE_SLAB:
        x = op(x.reshape(rows // REDUCE_SLAB, REDUCE_SLAB, cols), axis=0)
    return op(x, axis=0, keepdims=True)


def _mlp_kernel(h_ref, g_ref, w1_ref, w2_ref, gf_ref, o_ref, hn_ref, *, final):
    j = pl.program_id(1)

    def update(first):
        if first:
            hn_ref[...] = _rms(h_ref[...], g_ref[...]).astype(BF16)
        a = jnp.maximum(_dot(hn_ref[...], w1_ref[...]), 0.0)
        r = _dot((a * a).astype(BF16), w2_ref[...])
        if first:
            o_ref[...] = h_ref[...] + r
        else:
            o_ref[...] += r

    pl.when(j == 0)(functools.partial(update, True))
    pl.when(j > 0)(functools.partial(update, False))

    if final:
        @pl.when(j == pl.num_programs(1) - 1)
        def _():
            o_ref[...] = _rms(o_ref[...], gf_ref[...])


def _mlp(h, g, w1, w2, gf, *, layer, final, tm, tf):
    n, d = h.shape
    ff = w1.shape[2]
    return pl.pallas_call(
        functools.partial(_mlp_kernel, final=final),
        grid=(n // tm, ff // tf),
        in_specs=[
            pl.BlockSpec((tm, d), lambda i, j: (i, 0)),
            pl.BlockSpec((1, d), lambda i, j: (0, 0)),
            pl.BlockSpec((None, d, tf), lambda i, j: (layer, 0, j)),
            pl.BlockSpec((None, tf, d), lambda i, j: (layer, j, 0)),
            pl.BlockSpec((1, d), lambda i, j: (0, 0)),
        ],
        out_specs=pl.BlockSpec((tm, d), lambda i, j: (i, 0)),
        out_shape=jax.ShapeDtypeStruct((n, d), F32),
        scratch_shapes=[pltpu.VMEM((tm, d), BF16)],
        compiler_params=_params(2),
        name="sqrelu_mlp",
    )(h, g, w1, w2, gf)


def _conv_kernel(h_ref, g_ref, wb_ref, wc_ref, wx_ref, cw_ref, wo_ref, o_ref,
                 hn_ref, halo_ref, *, tiles_per_seq):
    i = pl.program_id(0)
    j = pl.program_id(1)

    @pl.when(j == 0)
    def _():
        h = h_ref[...]
        hn_ref[...] = _rms(h, g_ref[...]).astype(BF16)
        o_ref[...] = h

    hn = hn_ref[...]
    bg = _dot(hn, wb_ref[...])
    z = _dot(hn, wc_ref[...]) * _dot(hn, wx_ref[...])
    tm = z.shape[0]

    @pl.when(i % tiles_per_seq == 0)
    def _():
        halo_ref[j] = jnp.zeros(halo_ref.shape[1:], F32)

    prev = halo_ref[j]
    halo_ref[j] = z[tm - SUBLANES:, :]
    p1 = prev[SUBLANES - 1:SUBLANES, :]
    p2 = prev[SUBLANES - 2:SUBLANES - 1, :]

    row = lax.broadcasted_iota(jnp.int32, (tm, 1), 0)
    z1 = jnp.where(row == 0, p1, pltpu.roll(z, 1, 0))
    z2 = jnp.where(row == 0, p2, jnp.where(row == 1, p1, pltpu.roll(z, 2, 0)))
    cw = cw_ref[...]
    zc = cw[0:1, :] * z2 + cw[1:2, :] * z1 + cw[2:3, :] * z
    o_ref[...] += _dot((bg * zc).astype(BF16), wo_ref[...])


def _conv(h, g, w_in, cw, w_out, *, layer, seq, tm, tc):
    n, d = h.shape
    nc = d // tc
    return pl.pallas_call(
        functools.partial(_conv_kernel, tiles_per_seq=seq // tm),
        grid=(n // tm, nc),
        in_specs=[
            pl.BlockSpec((tm, d), lambda i, j: (i, 0)),
            pl.BlockSpec((1, d), lambda i, j: (0, 0)),
            pl.BlockSpec((None, d, tc), lambda i, j: (layer, 0, j)),
            pl.BlockSpec((None, d, tc), lambda i, j: (layer, 0, j + nc)),
            pl.BlockSpec((None, d, tc), lambda i, j: (layer, 0, j + 2 * nc)),
            pl.BlockSpec((SUBLANES, tc), lambda i, j: (0, j)),
            pl.BlockSpec((None, tc, d), lambda i, j: (layer, j, 0)),
        ],
        out_specs=pl.BlockSpec((tm, d), lambda i, j: (i, 0)),
        out_shape=jax.ShapeDtypeStruct((n, d), F32),
        scratch_shapes=[pltpu.VMEM((tm, d), BF16), pltpu.VMEM((nc, SUBLANES, tc), F32)],
        compiler_params=_params(2),
        name="short_conv",
    )(h, g, w_in, w_in, w_in, cw, w_out)


def _attn_proj_kernel(h_ref, g_ref, w_ref, qg_ref, kvg_ref, lng_ref, lnb_ref,
                      cqt_ref, ckv_ref, ckvt_ref, kidx_ref, widxt_ref,
                      *, q_lora, kv_lora, idx_dim, w_scale):
    hn = _rms(h_ref[...], g_ref[...]).astype(BF16)
    proj = _dot(hn, w_ref[...])
    cqt_ref[...] = _rms(proj[:, :q_lora], qg_ref[...]).T.astype(BF16)
    ckv = _rms(proj[:, q_lora:q_lora + kv_lora], kvg_ref[...])
    ckv_ref[...] = ckv.astype(BF16)
    ckvt_ref[...] = ckv.T.astype(BF16)

    o = q_lora + kv_lora
    kx = proj[:, o:o + LANES]
    valid = lax.broadcasted_iota(jnp.int32, (1, LANES), 1) < idx_dim
    mu = jnp.sum(kx, axis=-1, keepdims=True) * (1.0 / idx_dim)
    dlt = jnp.where(valid, kx - mu, 0.0)
    var = jnp.sum(dlt * dlt, axis=-1, keepdims=True) * (1.0 / idx_dim)
    kidx_ref[...] = (dlt * lax.rsqrt(var + EPS) * lng_ref[...] + lnb_ref[...]).astype(BF16)
    widxt_ref[...] = (proj[:, o + LANES:o + 2 * LANES] * w_scale).T


def _attn_proj(h, g, w, qg, kvg, lng, lnb, *, q_lora, kv_lora, idx_dim, w_scale, tm):
    n, d = h.shape
    cols = w.shape[1]
    row = lambda c: pl.BlockSpec((1, c), lambda i: (0, 0))
    return pl.pallas_call(
        functools.partial(_attn_proj_kernel, q_lora=q_lora, kv_lora=kv_lora,
                          idx_dim=idx_dim, w_scale=w_scale),
        grid=(n // tm,),
        in_specs=[
            pl.BlockSpec((tm, d), lambda i: (i, 0)),
            row(d),
            pl.BlockSpec((d, cols), lambda i: (0, 0)),
            row(q_lora), row(kv_lora), row(LANES), row(LANES),
        ],
        out_specs=[
            pl.BlockSpec((q_lora, tm), lambda i: (0, i)),
            pl.BlockSpec((tm, kv_lora), lambda i: (i, 0)),
            pl.BlockSpec((kv_lora, tm), lambda i: (0, i)),
            pl.BlockSpec((tm, LANES), lambda i: (i, 0)),
            pl.BlockSpec((LANES, tm), lambda i: (0, i)),
        ],
        out_shape=[
            jax.ShapeDtypeStruct((q_lora, n), BF16),
            jax.ShapeDtypeStruct((n, kv_lora), BF16),
            jax.ShapeDtypeStruct((kv_lora, n), BF16),
            jax.ShapeDtypeStruct((n, LANES), BF16),
            jax.ShapeDtypeStruct((LANES, n), F32),
        ],
        compiler_params=_params(1),
        name="attn_proj",
    )(h, g, w, qg, kvg, lng, lnb)


def _count(pred):
    return _reduce_rows(jnp.sum, pred.astype(F32))


def _attn_select(c, kidx_ref, qi_s, w_s, key_s, bias_s, thr_s, tie_s, *, n_heads, topk):
    seq, tq = key_s.shape
    s = (c + 1) * tq
    kpos = lax.broadcasted_iota(jnp.int32, (s, 1), 0)
    qpos = c * tq + lax.broadcasted_iota(jnp.int32, (1, tq), 1)
    causal = kpos <= qpos

    if s <= topk:
        bias_s[:s, :] = jnp.where(causal, 0.0, -jnp.inf)
        tie_s[0] = 0
        return

    def head_score(h):
        return w_s[h][0:1, :] * jnp.maximum(_dot(kidx_ref[:s, :], qi_s[h]), 0.0)

    def score_body(g, carry):
        h = g * SCORE_GROUP
        part = head_score(h)
        for d in range(1, SCORE_GROUP):
            part = part + head_score(h + d)
        bias_s[:s, :] += part
        return carry
    bias_s[:s, :] = jnp.zeros((s, tq), F32)
    lax.fori_loop(0, n_heads // SCORE_GROUP, score_body, 0)

    bits = pltpu.bitcast(bias_s[:s, :], jnp.int32)
    key = bits ^ ((bits >> 31) & 0x7FFFFFFF)
    key_s[:s, :] = jnp.where(causal, key, INT_MIN)
    if s < seq:
        key_s[s:, :] = jnp.full((seq - s, tq), INT_MIN, jnp.int32)

    def select_body(i, thr_u):
        cand_u = thr_u | jnp.left_shift(jnp.int32(1), KEY_BITS - 1 - i)
        ok = _count(key_s[:s, :] >= (cand_u ^ INT_MIN)) >= topk
        return jnp.where(ok, cand_u, thr_u)
    thr_u = lax.fori_loop(0, KEY_BITS, select_body, jnp.zeros((1, tq), jnp.int32))
    thr = thr_u ^ INT_MIN
    thr_s[...] = jnp.broadcast_to(thr, thr_s.shape)

    ge = key_s[:s, :] >= thr
    if c * tq < topk:
        ge = ge & causal
    bias_s[:s, :] = jnp.where(ge, 0.0, -jnp.inf)
    tie_s[0] = (jnp.max(_count(ge)) > topk).astype(jnp.int32)


def _attn_ties(t0, key_s, bias_s, thr_s, *, topk):
    seq, tq = key_s.shape
    kpos = lax.broadcasted_iota(jnp.int32, (seq, 1), 0)
    qpos = t0 + lax.broadcasted_iota(jnp.int32, (1, tq), 1)
    causal = kpos <= qpos
    key = key_s[...]
    thr = thr_s[0:1, :]
    gt = (key > thr) & causal
    eq = (key == thr) & causal
    keep = topk - _count(gt)
    nbits = seq.bit_length()

    def cut_body(i, cut):
        cand = cut | jnp.left_shift(jnp.int32(1), nbits - 1 - i)
        ok = _count(eq & (kpos < cand)) <= keep
        return jnp.where(ok, cand, cut)
    cut = lax.fori_loop(0, nbits, cut_body, jnp.zeros((1, tq), jnp.int32))
    bias_s[...] = jnp.where(gt | (eq & (kpos < cut)), 0.0, -jnp.inf)


def _attn_heads(c, ckv_ref, ckvt_ref, ql_s, bias_s, ol_s, lg_s, m_s, d_s, e_s, *, n_heads):
    tq = bias_s.shape[1]
    s = (c + 1) * tq

    def logits(q, b):
        lg = _dot(ckv_ref[:s, :], q) + bias_s[:s, :]
        lg_s[b, :s, :] = lg
        m_s[b] = jnp.broadcast_to(_reduce_rows(jnp.max, lg), (SUBLANES, tq))

    def softmax(b):
        e = jnp.exp2(lg_s[b, :s, :] - m_s[b][0:1, :])
        d_s[b] = jnp.broadcast_to(_reduce_rows(jnp.sum, e), (SUBLANES, tq))
        e_s[b, :s, :] = e.astype(BF16)

    def pv(b):
        return (_dot(ckvt_ref[:, :s], e_s[b, :s, :]) / d_s[b][0:1, :]).astype(BF16)

    def steps(t0, count):
        qs = [ql_s[t0 + u] for u in range(count)]
        outs = []
        for u in range(count):
            outs.append(pv(u % 2))
            softmax((u + 1) % 2)
            logits(qs[u], u % 2)
        for u in range(count):
            ol_s[t0 + u - 2] = outs[u]

    logits(ql_s[0], 0)
    softmax(0)
    logits(ql_s[1], 1)

    n_steady = n_heads - 2
    peeled = n_steady % PIPE_UNROLL
    if peeled:
        steps(2, peeled)
    if n_steady // PIPE_UNROLL > 1:
        def pipe_body(p, carry):
            steps(2 + peeled + PIPE_UNROLL * p, PIPE_UNROLL)
            return carry
        lax.fori_loop(0, n_steady // PIPE_UNROLL, pipe_body, 0)
    elif n_steady // PIPE_UNROLL == 1:
        steps(2 + peeled, PIPE_UNROLL)

    oa = pv(0)
    softmax(1)
    ol_s[n_heads - 2] = oa
    ol_s[n_heads - 1] = pv(1)


def _attn_kernel(cqt_ref, widxt_ref, kidx_ref, ckv_ref, ckvt_ref,
                 wuqt_ref, wukt_ref, wqit_ref, wuvt_ref, o_ref,
                 ql_s, qi_s, w_s, key_s, bias_s, ol_s, lg_s, m_s, d_s, e_s, thr_s, tie_s,
                 *, n_heads, topk, qscale):
    seq, tq = key_s.shape
    hd = wukt_ref.shape[2]
    vd = wuvt_ref.shape[1]
    cqt = cqt_ref[...]
    q_all = _dot(wuqt_ref[...], cqt).astype(BF16)
    qi_all = _dot(wqit_ref[...], cqt).astype(BF16)
    widxt = widxt_ref[...]
    for h in range(n_heads):
        q_lat = _dot(wukt_ref[h], q_all[h * hd:(h + 1) * hd, :]) * qscale
        ql_s[h] = q_lat.astype(BF16)
        qi_s[h] = qi_all[h * LANES:(h + 1) * LANES, :]
        w_s[h] = jnp.broadcast_to(widxt[h:h + 1, :], (SUBLANES, tq))

    qtile = pl.program_id(1)
    for c in range(seq // tq):
        pl.when(qtile == c)(functools.partial(
            _attn_select, c, kidx_ref, qi_s, w_s, key_s, bias_s, thr_s, tie_s,
            n_heads=n_heads, topk=topk))
    pl.when(tie_s[0] != 0)(functools.partial(
        _attn_ties, qtile * tq, key_s, bias_s, thr_s, topk=topk))
    for c in range(seq // tq):
        pl.when(qtile == c)(functools.partial(
            _attn_heads, c, ckv_ref, ckvt_ref, ql_s, bias_s, ol_s, lg_s, m_s, d_s, e_s,
            n_heads=n_heads))

    for h in range(n_heads):
        o_ref[:, h * vd:(h + 1) * vd] = _dot(wuvt_ref[h], ol_s[h]).T.astype(BF16)


def _attn(cqt, widxt, kidx, ckv, ckvt, wuqt, wukt, wqit, wuvt, *, batch, seq, topk, scale, tq):
    q_lora, n = cqt.shape
    n_heads, kv_lora, hd = wukt.shape
    vd = wuvt.shape[1]
    nq = seq // tq
    const2 = lambda a: pl.BlockSpec(a.shape, lambda b, q: (0, 0))
    const3 = lambda a: pl.BlockSpec(a.shape, lambda b, q: (0, 0, 0))
    return pl.pallas_call(
        functools.partial(_attn_kernel, n_heads=n_heads, topk=topk,
                          qscale=scale * LOG2_E),
        grid=(batch, nq),
        in_specs=[
            pl.BlockSpec((q_lora, tq), lambda b, q: (0, b * nq + q)),
            pl.BlockSpec((LANES, tq), lambda b, q: (0, b * nq + q)),
            pl.BlockSpec((seq, LANES), lambda b, q: (b, 0)),
            pl.BlockSpec((seq, kv_lora), lambda b, q: (b, 0)),
            pl.BlockSpec((kv_lora, seq), lambda b, q: (0, b)),
            const2(wuqt), const3(wukt), const2(wqit), const3(wuvt),
        ],
        out_specs=pl.BlockSpec((tq, n_heads * vd), lambda b, q: (b * nq + q, 0)),
        out_shape=jax.ShapeDtypeStruct((n, n_heads * vd), BF16),
        scratch_shapes=[
            pltpu.VMEM((n_heads, kv_lora, tq), BF16),
            pltpu.VMEM((n_heads, LANES, tq), BF16),
            pltpu.VMEM((n_heads, SUBLANES, tq), F32),
            pltpu.VMEM((seq, tq), jnp.int32),
            pltpu.VMEM((seq, tq), F32),
            pltpu.VMEM((n_heads, kv_lora, tq), BF16),
            pltpu.VMEM((2, seq, tq), F32),
            pltpu.VMEM((2, SUBLANES, tq), F32),
            pltpu.VMEM((2, SUBLANES, tq), F32),
            pltpu.VMEM((2, seq, tq), BF16),
            pltpu.VMEM((SUBLANES, tq), jnp.int32),
            pltpu.SMEM((1,), jnp.int32),
        ],
        compiler_params=_params(2),
        name="dsa_attention",
    )(cqt, widxt, kidx, ckv, ckvt, wuqt, wukt, wqit, wuvt)


def _out_proj_kernel(h_ref, o_ref, w_ref, out_ref):
    out_ref[...] = h_ref[...] + _dot(o_ref[...], w_ref[...])


def _out_proj(h, o, w, *, layer, tm):
    n, d = h.shape
    k = o.shape[1]
    return pl.pallas_call(
        _out_proj_kernel,
        grid=(n // tm,),
        in_specs=[
            pl.BlockSpec((tm, d), lambda i: (i, 0)),
            pl.BlockSpec((tm, k), lambda i: (i, 0)),
            pl.BlockSpec((None, k, d), lambda i: (layer, 0, 0)),
        ],
        out_specs=pl.BlockSpec((tm, d), lambda i: (i, 0)),
        out_shape=jax.ShapeDtypeStruct((n, d), F32),
        compiler_params=_params(1),
        name="attn_out_proj",
    )(h, o, w)


def _pad_last(a, width):
    return jnp.pad(a, [(0, 0)] * (a.ndim - 1) + [(0, width - a.shape[-1])])


def _tile(n, want):
    t = min(n, want)
    assert n % t == 0, (n, t)
    return t


def kernel(x, norm_mix, norm_mlp, mlp_w1, mlp_w2, conv_in, conv_w, conv_out, attn_in,
           q_norm, kv_norm, w_uq, w_uk, w_uv, w_qidx, kidx_ln_g, kidx_ln_b, attn_out,
           final_norm):
    batch, seq, d = x.shape
    depth = norm_mix.shape[0]
    q_lora, n_heads, hd = w_uq.shape[1:]
    kv_lora = w_uk.shape[3]
    idx_heads, idx_dim = w_qidx.shape[2:]
    assert idx_heads == n_heads and idx_dim <= LANES and n_heads <= LANES
    assert n_heads % SCORE_GROUP == 0 and n_heads % 2 == 0 and n_heads >= 4
    assert conv_w.shape[1] == CONV_WIDTH
    topk = min(TOPK_MAX, seq // 4)
    n = batch * seq

    tm = _tile(seq, 512)
    tf = _tile(mlp_w1.shape[2], 2048)
    tc = _tile(d, 512)
    tq = _tile(seq, 256)

    mlp_w1, mlp_w2, conv_in, conv_out, attn_out = (
        w.astype(BF16) for w in (mlp_w1, mlp_w2, conv_in, conv_out, attn_out))

    h = x.reshape(n, d)
    row = lambda v: v.reshape(1, -1)
    gf = row(final_norm)
    for i in range(depth):
        j = i // 2
        g = row(norm_mix[i])
        if i % 2 == 0:
            cw = jnp.pad(conv_w[j], ((0, SUBLANES - CONV_WIDTH), (0, 0)))
            h = _conv(h, g, conv_in, cw, conv_out, layer=j, seq=seq, tm=tm, tc=tc)
        else:
            o = q_lora + kv_lora
            w = jnp.concatenate([
                attn_in[j][:, :o],
                _pad_last(attn_in[j][:, o:o + idx_dim], LANES),
                _pad_last(attn_in[j][:, o + idx_dim:], LANES)], axis=1).astype(BF16)
            cqt, ckv, ckvt, kidx, widxt = _attn_proj(
                h, g, w, row(q_norm[j]), row(kv_norm[j]),
                _pad_last(row(kidx_ln_g[j]), LANES), _pad_last(row(kidx_ln_b[j]), LANES),
                q_lora=q_lora, kv_lora=kv_lora, idx_dim=idx_dim,
                w_scale=float(idx_heads ** -0.5 * idx_dim ** -0.5), tm=tm)
            wuqt = w_uq[j].reshape(q_lora, n_heads * hd).T.astype(BF16)
            wukt = w_uk[j].transpose(0, 2, 1).astype(BF16)
            wqit = _pad_last(w_qidx[j], LANES).reshape(q_lora, n_heads * LANES).T.astype(BF16)
            wuvt = w_uv[j].transpose(0, 2, 1).astype(BF16)
            oh = _attn(cqt, widxt, kidx, ckv, ckvt, wuqt, wukt, wqit, wuvt,
                       batch=batch, seq=seq, topk=topk, scale=float(hd ** -0.5), tq=tq)
            h = _out_proj(h, oh, attn_out, layer=j, tm=tm)
        h = _mlp(h, row(norm_mlp[i]), mlp_w1, mlp_w2, gf,
                 layer=i, final=(i == depth - 1), tm=tm, tf=tf)
    return h.reshape(batch, seq, d)
```

```python
import functools

import jax
import jax.numpy as jnp
from jax import lax
from jax.experimental import pallas as pl
from jax.experimental.pallas import tpu as pltpu

EPS = 1e-6
TOPK_MAX = 256
LANES = 128
SUBLANES = 8
VMEM_LIMIT_BYTES = 56 * 1024 * 1024
INT_MIN = -2 ** 31
KEY_BITS = 32
CONV_WIDTH = 3
LOG2_E = 1.4426950408889634
SCORE_GROUP = 4
REDUCE_SLAB = 64
PIPE_UNROLL = 2

F32 = jnp.float32
BF16 = jnp.bfloat16


def _params(n_axes):
    return pltpu.CompilerParams(
        dimension_semantics=("arbitrary",) * n_axes,
        vmem_limit_bytes=VMEM_LIMIT_BYTES)


def _rms(x, g):
    return x * lax.rsqrt(jnp.mean(x * x, axis=-1, keepdims=True) + EPS) * g


def _dot(a, b):
    return jnp.dot(a, b, preferred_element_type=F32)


def _reduce_rows(op, x):
    rows, cols = x.shape
    if rows % REDUCE_SLAB == 0 and rows > REDUCE_SLAB:
        x = op(x.reshape(rows // REDUCE_SLAB, REDUCE_SLAB, cols), axis=0)
    return op(x, axis=0, keepdims=True)


def _mlp_kernel(h_ref, g_ref, w1_ref, w2_ref, gf_ref, o_ref, hn_ref, *, final):
    j = pl.program_id(1)

    def update(first):
        if first:
            hn_ref[...] = _rms(h_ref[...], g_ref[...]).astype(BF16)
        a = jnp.maximum(_dot(hn_ref[...], w1_ref[...]), 0.0)
        r = _dot((a * a).astype(BF16), w2_ref[...])
        if first:
            o_ref[...] = h_ref[...] + r
        else:
            o_ref[...] += r

    pl.when(j == 0)(functools.partial(update, True))
    pl.when(j > 0)(functools.partial(update, False))

    if final:
        @pl.when(j == pl.num_programs(1) - 1)
        def _():
            o_ref[...] = _rms(o_ref[...], gf_ref[...])


def _mlp(h, g, w1, w2, gf, *, layer, final, tm, tf):
    n, d = h.shape
    ff = w1.shape[2]
    return pl.pallas_call(
        functools.partial(_mlp_kernel, final=final),
        grid=(n // tm, ff // tf),
        in_specs=[
            pl.BlockSpec((tm, d), lambda i, j: (i, 0)),
            pl.BlockSpec((1, d), lambda i, j: (0, 0)),
            pl.BlockSpec((None, d, tf), lambda i, j: (layer, 0, j)),
            pl.BlockSpec((None, tf, d), lambda i, j: (layer, j, 0)),
            pl.BlockSpec((1, d), lambda i, j: (0, 0)),
        ],
        out_specs=pl.BlockSpec((tm, d), lambda i, j: (i, 0)),
        out_shape=jax.ShapeDtypeStruct((n, d), F32),
        scratch_shapes=[pltpu.VMEM((tm, d), BF16)],
        compiler_params=_params(2),
        name="sqrelu_mlp",
    )(h, g, w1, w2, gf)


def _conv_kernel(h_ref, g_ref, wb_ref, wc_ref, wx_ref, cw_ref, wo_ref, o_ref,
                 hn_ref, halo_ref, *, tiles_per_seq):
    i = pl.program_id(0)
    j = pl.program_id(1)

    @pl.when(j == 0)
    def _():
        h = h_ref[...]
        hn_ref[...] = _rms(h, g_ref[...]).astype(BF16)
        o_ref[...] = h

    hn = hn_ref[...]
    bg = _dot(hn, wb_ref[...])
    z = _dot(hn, wc_ref[...]) * _dot(hn, wx_ref[...])
    tm = z.shape[0]

    @pl.when(i % tiles_per_seq == 0)
    def _():
        halo_ref[j] = jnp.zeros(halo_ref.shape[1:], F32)

    prev = halo_ref[j]
    halo_ref[j] = z[tm - SUBLANES:, :]
    p1 = prev[SUBLANES - 1:SUBLANES, :]
    p2 = prev[SUBLANES - 2:SUBLANES - 1, :]

    row = lax.broadcasted_iota(jnp.int32, (tm, 1), 0)
    z1 = jnp.where(row == 0, p1, pltpu.roll(z, 1, 0))
    z2 = jnp.where(row == 0, p2, jnp.where(row == 1, p1, pltpu.roll(z, 2, 0)))
    cw = cw_ref[...]
    zc = cw[0:1, :] * z2 + cw[1:2, :] * z1 + cw[2:3, :] * z
    o_ref[...] += _dot((bg * zc).astype(BF16), wo_ref[...])


def _conv(h, g, w_in, cw, w_out, *, layer, seq, tm, tc):
    n, d = h.shape
    nc = d // tc
    return pl.pallas_call(
        functools.partial(_conv_kernel, tiles_per_seq=seq // tm),
        grid=(n // tm, nc),
        in_specs=[
            pl.BlockSpec((tm, d), lambda i, j: (i, 0)),
            pl.BlockSpec((1, d), lambda i, j: (0, 0)),
            pl.BlockSpec((None, d, tc), lambda i, j: (layer, 0, j)),
            pl.BlockSpec((None, d, tc), lambda i, j: (layer, 0, j + nc)),
            pl.BlockSpec((None, d, tc), lambda i, j: (layer, 0, j + 2 * nc)),
            pl.BlockSpec((SUBLANES, tc), lambda i, j: (0, j)),
            pl.BlockSpec((None, tc, d), lambda i, j: (layer, j, 0)),
        ],
        out_specs=pl.BlockSpec((tm, d), lambda i, j: (i, 0)),
        out_shape=jax.ShapeDtypeStruct((n, d), F32),
        scratch_shapes=[pltpu.VMEM((tm, d), BF16), pltpu.VMEM((nc, SUBLANES, tc), F32)],
        compiler_params=_params(2),
        name="short_conv",
    )(h, g, w_in, w_in, w_in, cw, w_out)


def _attn_proj_kernel(h_ref, g_ref, w_ref, qg_ref, kvg_ref, lng_ref, lnb_ref,
                      cqt_ref, ckv_ref, ckvt_ref, kidx_ref, widxt_ref,
                      *, q_lora, kv_lora, idx_dim, w_scale):
    hn = _rms(h_ref[...], g_ref[...]).astype(BF16)
    proj = _dot(hn, w_ref[...])
    cqt_ref[...] = _rms(proj[:, :q_lora], qg_ref[...]).T.astype(BF16)
    ckv = _rms(proj[:, q_lora:q_lora + kv_lora], kvg_ref[...])
    ckv_ref[...] = ckv.astype(BF16)
    ckvt_ref[...] = ckv.T.astype(BF16)

    o = q_lora + kv_lora
    kx = proj[:, o:o + LANES]
    valid = lax.broadcasted_iota(jnp.int32, (1, LANES), 1) < idx_dim
    mu = jnp.sum(kx, axis=-1, keepdims=True) * (1.0 / idx_dim)
    dlt = jnp.where(valid, kx - mu, 0.0)
    var = jnp.sum(dlt * dlt, axis=-1, keepdims=True) * (1.0 / idx_dim)
    kidx_ref[...] = (dlt * lax.rsqrt(var + EPS) * lng_ref[...] + lnb_ref[...]).astype(BF16)
    widxt_ref[...] = (proj[:, o + LANES:o + 2 * LANES] * w_scale).T


def _attn_proj(h, g, w, qg, kvg, lng, lnb, *, q_lora, kv_lora, idx_dim, w_scale, tm):
    n, d = h.shape
    cols = w.shape[1]
    row = lambda c: pl.BlockSpec((1, c), lambda i: (0, 0))
    return pl.pallas_call(
        functools.partial(_attn_proj_kernel, q_lora=q_lora, kv_lora=kv_lora,
                          idx_dim=idx_dim, w_scale=w_scale),
        grid=(n // tm,),
        in_specs=[
            pl.BlockSpec((tm, d), lambda i: (i, 0)),
            row(d),
            pl.BlockSpec((d, cols), lambda i: (0, 0)),
            row(q_lora), row(kv_lora), row(LANES), row(LANES),
        ],
        out_specs=[
            pl.BlockSpec((q_lora, tm), lambda i: (0, i)),
            pl.BlockSpec((tm, kv_lora), lambda i: (i, 0)),
            pl.BlockSpec((kv_lora, tm), lambda i: (0, i)),
            pl.BlockSpec((tm, LANES), lambda i: (i, 0)),
            pl.BlockSpec((LANES, tm), lambda i: (0, i)),
        ],
        out_shape=[
            jax.ShapeDtypeStruct((q_lora, n), BF16),
            jax.ShapeDtypeStruct((n, kv_lora), BF16),
            jax.ShapeDtypeStruct((kv_lora, n), BF16),
            jax.ShapeDtypeStruct((n, LANES), BF16),
            jax.ShapeDtypeStruct((LANES, n), F32),
        ],
        compiler_params=_params(1),
        name="attn_proj",
    )(h, g, w, qg, kvg, lng, lnb)


def _count(pred):
    return _reduce_rows(jnp.sum, pred.astype(F32))


def _attn_select(c, kidx_ref, qi_s, w_s, key_s, bias_s, thr_s, tie_s, *, n_heads, topk):
    seq, tq = key_s.shape
    s = (c + 1) * tq
    kpos = lax.broadcasted_iota(jnp.int32, (s, 1), 0)
    qpos = c * tq + lax.broadcasted_iota(jnp.int32, (1, tq), 1)
    causal = kpos <= qpos

    if s <= topk:
        bias_s[:s, :] = jnp.where(causal, 0.0, -jnp.inf)
        tie_s[0] = 0
        return

    def head_score(h):
        return w_s[h][0:1, :] * jnp.maximum(_dot(kidx_ref[:s, :], qi_s[h]), 0.0)

    def score_body(g, carry):
        h = g * SCORE_GROUP
        part = head_score(h)
        for d in range(1, SCORE_GROUP):
            part = part + head_score(h + d)
        bias_s[:s, :] += part
        return carry
    bias_s[:s, :] = jnp.zeros((s, tq), F32)
    lax.fori_loop(0, n_heads // SCORE_GROUP, score_body, 0)

    bits = pltpu.bitcast(bias_s[:s, :], jnp.int32)
    key = bits ^ ((bits >> 31) & 0x7FFFFFFF)
    key_s[:s, :] = jnp.where(causal, key, INT_MIN)
    if s < seq:
        key_s[s:, :] = jnp.full((seq - s, tq), INT_MIN, jnp.int32)

    def select_body(i, thr_u):
        cand_u = thr_u | jnp.left_shift(jnp.int32(1), KEY_BITS - 1 - i)
        ok = _count(key_s[:s, :] >= (cand_u ^ INT_MIN)) >= topk
        return jnp.where(ok, cand_u, thr_u)
    thr_u = lax.fori_loop(0, KEY_BITS, select_body, jnp.zeros((1, tq), jnp.int32))
    thr = thr_u ^ INT_MIN
    thr_s[...] = jnp.broadcast_to(thr, thr_s.shape)

    ge = key_s[:s, :] >= thr
    if c * tq < topk:
        ge = ge & causal
    bias_s[:s, :] = jnp.where(ge, 0.0, -jnp.inf)
    tie_s[0] = (jnp.max(_count(ge)) > topk).astype(jnp.int32)


def _attn_ties(t0, key_s, bias_s, thr_s, *, topk):
    seq, tq = key_s.shape
    kpos = lax.broadcasted_iota(jnp.int32, (seq, 1), 0)
    qpos = t0 + lax.broadcasted_iota(jnp.int32, (1, tq), 1)
    causal = kpos <= qpos
    key = key_s[...]
    thr = thr_s[0:1, :]
    gt = (key > thr) & causal
    eq = (key == thr) & causal
    keep = topk - _count(gt)
    nbits = seq.bit_length()

    def cut_body(i, cut):
        cand = cut | jnp.left_shift(jnp.int32(1), nbits - 1 - i)
        ok = _count(eq & (kpos < cand)) <= keep
        return jnp.where(ok, cand, cut)
    cut = lax.fori_loop(0, nbits, cut_body, jnp.zeros((1, tq), jnp.int32))
    bias_s[...] = jnp.where(gt | (eq & (kpos < cut)), 0.0, -jnp.inf)


def _attn_heads(c, ckv_ref, ckvt_ref, ql_s, bias_s, ol_s, lg_s, m_s, d_s, e_s, *, n_heads):
    tq = bias_s.shape[1]
    s = (c + 1) * tq

    def logits(q, b):
        lg = _dot(ckv_ref[:s, :], q) + bias_s[:s, :]
        lg_s[b, :s, :] = lg
        m_s[b] = jnp.broadcast_to(_reduce_rows(jnp.max, lg), (SUBLANES, tq))

    def softmax(b):
        e = jnp.exp2(lg_s[b, :s, :] - m_s[b][0:1, :])
        d_s[b] = jnp.broadcast_to(_reduce_rows(jnp.sum, e), (SUBLANES, tq))
        e_s[b, :s, :] = e.astype(BF16)

    def pv(b):
        return (_dot(ckvt_ref[:, :s], e_s[b, :s, :]) / d_s[b][0:1, :]).astype(BF16)

    def steps(t0, count):
        qs = [ql_s[t0 + u] for u in range(count)]
        outs = []
        for u in range(count):
            outs.append(pv(u % 2))
            softmax((u + 1) % 2)
            logits(qs[u], u % 2)
        for u in range(count):
            ol_s[t0 + u - 2] = outs[u]

    logits(ql_s[0], 0)
    softmax(0)
    logits(ql_s[1], 1)

    n_steady = n_heads - 2
    peeled = n_steady % PIPE_UNROLL
    if peeled:
        steps(2, peeled)
    if n_steady // PIPE_UNROLL > 1:
        def pipe_body(p, carry):
            steps(2 + peeled + PIPE_UNROLL * p, PIPE_UNROLL)
            return carry
        lax.fori_loop(0, n_steady // PIPE_UNROLL, pipe_body, 0)
    elif n_steady // PIPE_UNROLL == 1:
        steps(2 + peeled, PIPE_UNROLL)

    oa = pv(0)
    softmax(1)
    ol_s[n_heads - 2] = oa
    ol_s[n_heads - 1] = pv(1)


def _mask_kernel(cqt_ref, widxt_ref, kidx_ref, wqit_ref, mask_ref,
                 qi_s, w_s, key_s, bias_s, thr_s, tie_s, *, n_heads, topk):
    seq, tq = key_s.shape
    qi_all = _dot(wqit_ref[...], cqt_ref[...]).astype(BF16)
    widxt = widxt_ref[...]
    for h in range(n_heads):
        qi_s[h] = qi_all[h * LANES:(h + 1) * LANES, :]
        w_s[h] = jnp.broadcast_to(widxt[h:h + 1, :], (SUBLANES, tq))

    bias_s[...] = jnp.full((seq, tq), -jnp.inf, F32)
    qtile = pl.program_id(1)
    for c in range(seq // tq):
        pl.when(qtile == c)(functools.partial(
            _attn_select, c, kidx_ref, qi_s, w_s, key_s, bias_s, thr_s, tie_s,
            n_heads=n_heads, topk=topk))
    pl.when(tie_s[0] != 0)(functools.partial(
        _attn_ties, qtile * tq, key_s, bias_s, thr_s, topk=topk))
    mask_ref[...] = bias_s[...].astype(BF16)


def _heads_kernel(cqt_ref, mask_ref, ckv_ref, ckvt_ref, wuqt_ref, wukt_ref, wuvt_ref, o_ref,
                  ql_s, bias_s, ol_s, lg_s, m_s, d_s, e_s, *, n_heads, qscale):
    seq, tq = bias_s.shape
    hd = wukt_ref.shape[2]
    vd = wuvt_ref.shape[1]
    q_all = _dot(wuqt_ref[...], cqt_ref[...]).astype(BF16)
    for h in range(n_heads):
        q_lat = _dot(wukt_ref[h], q_all[h * hd:(h + 1) * hd, :]) * qscale
        ql_s[h] = q_lat.astype(BF16)
    bias_s[...] = mask_ref[...].astype(F32)

    qtile = pl.program_id(1)
    for c in range(seq // tq):
        pl.when(qtile == c)(functools.partial(
            _attn_heads, c, ckv_ref, ckvt_ref, ql_s, bias_s, ol_s, lg_s, m_s, d_s, e_s,
            n_heads=n_heads))

    for h in range(n_heads):
        o_ref[:, h * vd:(h + 1) * vd] = _dot(wuvt_ref[h], ol_s[h]).T.astype(BF16)


def _attn(cqt, widxt, kidx, ckv, ckvt, wuqt, wukt, wqit, wuvt, *, batch, seq, topk, scale, tq):
    q_lora, n = cqt.shape
    n_heads, kv_lora, hd = wukt.shape
    vd = wuvt.shape[1]
    nq = seq // tq
    const2 = lambda a: pl.BlockSpec(a.shape, lambda b, q: (0, 0))
    const3 = lambda a: pl.BlockSpec(a.shape, lambda b, q: (0, 0, 0))
    qcols = lambda rows: pl.BlockSpec((rows, tq), lambda b, q: (0, b * nq + q))
    mask = pl.pallas_call(
        functools.partial(_mask_kernel, n_heads=n_heads, topk=topk),
        grid=(batch, nq),
        in_specs=[qcols(q_lora), qcols(LANES),
                  pl.BlockSpec((seq, LANES), lambda b, q: (b, 0)), const2(wqit)],
        out_specs=qcols(seq),
        out_shape=jax.ShapeDtypeStruct((seq, n), BF16),
        scratch_shapes=[
            pltpu.VMEM((n_heads, LANES, tq), BF16),
            pltpu.VMEM((n_heads, SUBLANES, tq), F32),
            pltpu.VMEM((seq, tq), jnp.int32),
            pltpu.VMEM((seq, tq), F32),
            pltpu.VMEM((SUBLANES, tq), jnp.int32),
            pltpu.SMEM((1,), jnp.int32),
        ],
        compiler_params=_params(2),
        name="dsa_topk_mask",
    )(cqt, widxt, kidx, wqit)
    return pl.pallas_call(
        functools.partial(_heads_kernel, n_heads=n_heads, qscale=scale * LOG2_E),
        grid=(batch, nq),
        in_specs=[
            qcols(q_lora), qcols(seq),
            pl.BlockSpec((seq, kv_lora), lambda b, q: (b, 0)),
            pl.BlockSpec((kv_lora, seq), lambda b, q: (0, b)),
            const2(wuqt), const3(wukt), const3(wuvt),
        ],
        out_specs=pl.BlockSpec((tq, n_heads * vd), lambda b, q: (b * nq + q, 0)),
        out_shape=jax.ShapeDtypeStruct((n, n_heads * vd), BF16),
        scratch_shapes=[
            pltpu.VMEM((n_heads, kv_lora, tq), BF16),
            pltpu.VMEM((seq, tq), F32),
            pltpu.VMEM((n_heads, kv_lora, tq), BF16),
            pltpu.VMEM((2, seq, tq), F32),
            pltpu.VMEM((2, SUBLANES, tq), F32),
            pltpu.VMEM((2, SUBLANES, tq), F32),
            pltpu.VMEM((2, seq, tq), BF16),
        ],
        compiler_params=_params(2),
        name="dsa_attention",
    )(cqt, mask, ckv, ckvt, wuqt, wukt, wuvt)


def _out_proj_kernel(h_ref, o_ref, w_ref, out_ref):
    out_ref[...] = h_ref[...] + _dot(o_ref[...], w_ref[...])


def _out_proj(h, o, w, *, layer, tm):
    n, d = h.shape
    k = o.shape[1]
    return pl.pallas_call(
        _out_proj_kernel,
        grid=(n // tm,),
        in_specs=[
            pl.BlockSpec((tm, d), lambda i: (i, 0)),
            pl.BlockSpec((tm, k), lambda i: (i, 0)),
            pl.BlockSpec((None, k, d), lambda i: (layer, 0, 0)),
        ],
        out_specs=pl.BlockSpec((tm, d), lambda i: (i, 0)),
        out_shape=jax.ShapeDtypeStruct((n, d), F32),
        compiler_params=_params(1),
        name="attn_out_proj",
    )(h, o, w)


def _pad_last(a, width):
    return jnp.pad(a, [(0, 0)] * (a.ndim - 1) + [(0, width - a.shape[-1])])


def _tile(n, want):
    t = min(n, want)
    assert n % t == 0, (n, t)
    return t


def kernel(x, norm_mix, norm_mlp, mlp_w1, mlp_w2, conv_in, conv_w, conv_out, attn_in,
           q_norm, kv_norm, w_uq, w_uk, w_uv, w_qidx, kidx_ln_g, kidx_ln_b, attn_out,
           final_norm):
    batch, seq, d = x.shape
    depth = norm_mix.shape[0]
    q_lora, n_heads, hd = w_uq.shape[1:]
    kv_lora = w_uk.shape[3]
    idx_heads, idx_dim = w_qidx.shape[2:]
    assert idx_heads == n_heads and idx_dim <= LANES and n_heads <= LANES
    assert n_heads % SCORE_GROUP == 0 and n_heads % 2 == 0 and n_heads >= 4
    assert conv_w.shape[1] == CONV_WIDTH
    topk = min(TOPK_MAX, seq // 4)
    n = batch * seq

    tm = _tile(seq, 512)
    tf = _tile(mlp_w1.shape[2], 2048)
    tc = _tile(d, 512)
    tq = _tile(seq, 256)

    mlp_w1, mlp_w2, conv_in, conv_out, attn_out = (
        w.astype(BF16) for w in (mlp_w1, mlp_w2, conv_in, conv_out, attn_out))

    h = x.reshape(n, d)
    row = lambda v: v.reshape(1, -1)
    gf = row(final_norm)
    for i in range(depth):
        j = i // 2
        g = row(norm_mix[i])
        if i % 2 == 0:
            cw = jnp.pad(conv_w[j], ((0, SUBLANES - CONV_WIDTH), (0, 0)))
            h = _conv(h, g, conv_in, cw, conv_out, layer=j, seq=seq, tm=tm, tc=tc)
        else:
            o = q_lora + kv_lora
            w = jnp.concatenate([
                attn_in[j][:, :o],
                _pad_last(attn_in[j][:, o:o + idx_dim], LANES),
                _pad_last(attn_in[j][:, o + idx_dim:], LANES)], axis=1).astype(BF16)
            cqt, ckv, ckvt, kidx, widxt = _attn_proj(
                h, g, w, row(q_norm[j]), row(kv_norm[j]),
                _pad_last(row(kidx_ln_g[j]), LANES), _pad_last(row(kidx_ln_b[j]), LANES),
                q_lora=q_lora, kv_lora=kv_lora, idx_dim=idx_dim,
                w_scale=float(idx_heads ** -0.5 * idx_dim ** -0.5), tm=tm)
            wuqt = w_uq[j].reshape(q_lora, n_heads * hd).T.astype(BF16)
            wukt = w_uk[j].transpose(0, 2, 1).astype(BF16)
            wqit = _pad_last(w_qidx[j], LANES).reshape(q_lora, n_heads * LANES).T.astype(BF16)
            wuvt = w_uv[j].transpose(0, 2, 1).astype(BF16)
            oh = _attn(cqt, widxt, kidx, ckv, ckvt, wuqt, wukt, wqit, wuvt,
                       batch=batch, seq=seq, topk=topk, scale=float(hd ** -0.5), tq=tq)
            h = _out_proj(h, oh, attn_out, layer=j, tm=tm)
        h = _mlp(h, row(norm_mlp[i]), mlp_w1, mlp_w2, gf,
                 layer=i, final=(i == depth - 1), tm=tm, tf=tf)
    return h.reshape(batch, seq, d)
```

```python
import functools

import jax
import jax.numpy as jnp
from jax import lax
from jax.experimental import pallas as pl
from jax.experimental.pallas import tpu as pltpu

EPS = 1e-6
TOPK_MAX = 256
LANES = 128
SUBLANES = 8
VMEM_LIMIT_BYTES = 56 * 1024 * 1024
INT_MIN = -2 ** 31
KEY_BITS = 32
CONV_WIDTH = 3
LOG2_E = 1.4426950408889634
SCORE_GROUP = 8
REDUCE_SLAB = 64
PIPE_UNROLL = 2

F32 = jnp.float32
BF16 = jnp.bfloat16


def _params(n_axes):
    return pltpu.CompilerParams(
        dimension_semantics=("arbitrary",) * n_axes,
        vmem_limit_bytes=VMEM_LIMIT_BYTES)


def _rms(x, g):
    return x * lax.rsqrt(jnp.mean(x * x, axis=-1, keepdims=True) + EPS) * g


def _dot(a, b):
    return jnp.dot(a, b, preferred_element_type=F32)


def _reduce_rows(op, x):
    rows, cols = x.shape
    if rows % REDUCE_SLAB == 0 and rows > REDUCE_SLAB:
        x = op(x.reshape(rows // REDUCE_SLAB, REDUCE_SLAB, cols), axis=0)
    return op(x, axis=0, keepdims=True)


def _mlp_kernel(h_ref, g_ref, w1_ref, w2_ref, gf_ref, o_ref, hn_ref, *, final):
    j = pl.program_id(1)

    def update(first):
        if first:
            hn_ref[...] = _rms(h_ref[...], g_ref[...]).astype(BF16)
        a = jnp.maximum(_dot(hn_ref[...], w1_ref[...]), 0.0)
        r = _dot((a * a).astype(BF16), w2_ref[...])
        if first:
            o_ref[...] = h_ref[...] + r
        else:
            o_ref[...] += r

    pl.when(j == 0)(functools.partial(update, True))
    pl.when(j > 0)(functools.partial(update, False))

    if final:
        @pl.when(j == pl.num_programs(1) - 1)
        def _():
            o_ref[...] = _rms(o_ref[...], gf_ref[...])


def _mlp(h, g, w1, w2, gf, *, layer, final, tm, tf):
    n, d = h.shape
    ff = w1.shape[2]
    return pl.pallas_call(
        functools.partial(_mlp_kernel, final=final),
        grid=(n // tm, ff // tf),
        in_specs=[
            pl.BlockSpec((tm, d), lambda i, j: (i, 0)),
            pl.BlockSpec((1, d), lambda i, j: (0, 0)),
            pl.BlockSpec((None, d, tf), lambda i, j: (layer, 0, j)),
            pl.BlockSpec((None, tf, d), lambda i, j: (layer, j, 0)),
            pl.BlockSpec((1, d), lambda i, j: (0, 0)),
        ],
        out_specs=pl.BlockSpec((tm, d), lambda i, j: (i, 0)),
        out_shape=jax.ShapeDtypeStruct((n, d), F32),
        scratch_shapes=[pltpu.VMEM((tm, d), BF16)],
        compiler_params=_params(2),
        name="sqrelu_mlp",
    )(h, g, w1, w2, gf)


def _conv_kernel(h_ref, g_ref, wb_ref, wc_ref, wx_ref, cw_ref, wo_ref, o_ref,
                 hn_ref, halo_ref, *, tiles_per_seq):
    i = pl.program_id(0)
    j = pl.program_id(1)

    @pl.when(j == 0)
    def _():
        h = h_ref[...]
        hn_ref[...] = _rms(h, g_ref[...]).astype(BF16)
        o_ref[...] = h

    hn = hn_ref[...]
    bg = _dot(hn, wb_ref[...])
    z = _dot(hn, wc_ref[...]) * _dot(hn, wx_ref[...])
    tm = z.shape[0]

    @pl.when(i % tiles_per_seq == 0)
    def _():
        halo_ref[j] = jnp.zeros(halo_ref.shape[1:], F32)

    prev = halo_ref[j]
    halo_ref[j] = z[tm - SUBLANES:, :]
    p1 = prev[SUBLANES - 1:SUBLANES, :]
    p2 = prev[SUBLANES - 2:SUBLANES - 1, :]

    row = lax.broadcasted_iota(jnp.int32, (tm, 1), 0)
    z1 = jnp.where(row == 0, p1, pltpu.roll(z, 1, 0))
    z2 = jnp.where(row == 0, p2, jnp.where(row == 1, p1, pltpu.roll(z, 2, 0)))
    cw = cw_ref[...]
    zc = cw[0:1, :] * z2 + cw[1:2, :] * z1 + cw[2:3, :] * z
    o_ref[...] += _dot((bg * zc).astype(BF16), wo_ref[...])


def _conv(h, g, w_in, cw, w_out, *, layer, seq, tm, tc):
    n, d = h.shape
    nc = d // tc
    return pl.pallas_call(
        functools.partial(_conv_kernel, tiles_per_seq=seq // tm),
        grid=(n // tm, nc),
        in_specs=[
            pl.BlockSpec((tm, d), lambda i, j: (i, 0)),
            pl.BlockSpec((1, d), lambda i, j: (0, 0)),
            pl.BlockSpec((None, d, tc), lambda i, j: (layer, 0, j)),
            pl.BlockSpec((None, d, tc), lambda i, j: (layer, 0, j + nc)),
            pl.BlockSpec((None, d, tc), lambda i, j: (layer, 0, j + 2 * nc)),
            pl.BlockSpec((SUBLANES, tc), lambda i, j: (0, j)),
            pl.BlockSpec((None, tc, d), lambda i, j: (layer, j, 0)),
        ],
        out_specs=pl.BlockSpec((tm, d), lambda i, j: (i, 0)),
        out_shape=jax.ShapeDtypeStruct((n, d), F32),
        scratch_shapes=[pltpu.VMEM((tm, d), BF16), pltpu.VMEM((nc, SUBLANES, tc), F32)],
        compiler_params=_params(2),
        name="short_conv",
    )(h, g, w_in, w_in, w_in, cw, w_out)


def _attn_proj_kernel(h_ref, g_ref, w_ref, qg_ref, kvg_ref, lng_ref, lnb_ref,
                      cqt_ref, ckv_ref, ckvt_ref, kidx_ref, widxt_ref,
                      *, q_lora, kv_lora, idx_dim, w_scale):
    hn = _rms(h_ref[...], g_ref[...]).astype(BF16)
    proj = _dot(hn, w_ref[...])
    cqt_ref[...] = _rms(proj[:, :q_lora], qg_ref[...]).T.astype(BF16)
    ckv = _rms(proj[:, q_lora:q_lora + kv_lora], kvg_ref[...])
    ckv_ref[...] = ckv.astype(BF16)
    ckvt_ref[...] = ckv.T.astype(BF16)

    o = q_lora + kv_lora
    kx = proj[:, o:o + LANES]
    valid = lax.broadcasted_iota(jnp.int32, (1, LANES), 1) < idx_dim
    mu = jnp.sum(kx, axis=-1, keepdims=True) * (1.0 / idx_dim)
    dlt = jnp.where(valid, kx - mu, 0.0)
    var = jnp.sum(dlt * dlt, axis=-1, keepdims=True) * (1.0 / idx_dim)
    kidx_ref[...] = (dlt * lax.rsqrt(var + EPS) * lng_ref[...] + lnb_ref[...]).astype(BF16)
    widxt_ref[...] = (proj[:, o + LANES:o + 2 * LANES] * w_scale).T


def _attn_proj(h, g, w, qg, kvg, lng, lnb, *, q_lora, kv_lora, idx_dim, w_scale, tm):
    n, d = h.shape
    cols = w.shape[1]
    row = lambda c: pl.BlockSpec((1, c), lambda i: (0, 0))
    return pl.pallas_call(
        functools.partial(_attn_proj_kernel, q_lora=q_lora, kv_lora=kv_lora,
                          idx_dim=idx_dim, w_scale=w_scale),
        grid=(n // tm,),
        in_specs=[
            pl.BlockSpec((tm, d), lambda i: (i, 0)),
            row(d),
            pl.BlockSpec((d, cols), lambda i: (0, 0)),
            row(q_lora), row(kv_lora), row(LANES), row(LANES),
        ],
        out_specs=[
            pl.BlockSpec((q_lora, tm), lambda i: (0, i)),
            pl.BlockSpec((tm, kv_lora), lambda i: (i, 0)),
            pl.BlockSpec((kv_lora, tm), lambda i: (0, i)),
            pl.BlockSpec((tm, LANES), lambda i: (i, 0)),
            pl.BlockSpec((LANES, tm), lambda i: (0, i)),
        ],
        out_shape=[
            jax.ShapeDtypeStruct((q_lora, n), BF16),
            jax.ShapeDtypeStruct((n, kv_lora), BF16),
            jax.ShapeDtypeStruct((kv_lora, n), BF16),
            jax.ShapeDtypeStruct((n, LANES), BF16),
            jax.ShapeDtypeStruct((LANES, n), F32),
        ],
        compiler_params=_params(1),
        name="attn_proj",
    )(h, g, w, qg, kvg, lng, lnb)


def _count(pred):
    return _reduce_rows(jnp.sum, pred.astype(F32))


def _attn_select(c, kidx_ref, qi_s, w_s, key_s, bias_s, thr_s, tie_s, *, n_heads, topk):
    seq, tq = key_s.shape
    s = (c + 1) * tq
    kpos = lax.broadcasted_iota(jnp.int32, (s, 1), 0)
    qpos = c * tq + lax.broadcasted_iota(jnp.int32, (1, tq), 1)
    causal = kpos <= qpos

    if s <= topk:
        bias_s[:s, :] = jnp.where(causal, 0.0, -jnp.inf)
        tie_s[0] = 0
        return

    def head_score(h):
        return w_s[h][0:1, :] * jnp.maximum(_dot(kidx_ref[:s, :], qi_s[h]), 0.0)

    def score_body(g, carry):
        h = g * SCORE_GROUP
        part = head_score(h)
        for d in range(1, SCORE_GROUP):
            part = part + head_score(h + d)
        bias_s[:s, :] += part
        return carry
    bias_s[:s, :] = jnp.zeros((s, tq), F32)
    lax.fori_loop(0, n_heads // SCORE_GROUP, score_body, 0)

    bits = pltpu.bitcast(bias_s[:s, :], jnp.int32)
    key = bits ^ ((bits >> 31) & 0x7FFFFFFF)
    key_s[:s, :] = jnp.where(causal, key, INT_MIN)
    if s < seq:
        key_s[s:, :] = jnp.full((seq - s, tq), INT_MIN, jnp.int32)

    def select_body(i, thr_u):
        cand_u = thr_u | jnp.left_shift(jnp.int32(1), KEY_BITS - 1 - i)
        ok = _count(key_s[:s, :] >= (cand_u ^ INT_MIN)) >= topk
        return jnp.where(ok, cand_u, thr_u)
    thr_u = lax.fori_loop(0, KEY_BITS, select_body, jnp.zeros((1, tq), jnp.int32))
    thr = thr_u ^ INT_MIN
    thr_s[...] = jnp.broadcast_to(thr, thr_s.shape)

    ge = key_s[:s, :] >= thr
    if c * tq < topk:
        ge = ge & causal
    bias_s[:s, :] = jnp.where(ge, 0.0, -jnp.inf)
    tie_s[0] = (jnp.max(_count(ge)) > topk).astype(jnp.int32)


def _attn_ties(t0, key_s, bias_s, thr_s, *, topk):
    seq, tq = key_s.shape
    kpos = lax.broadcasted_iota(jnp.int32, (seq, 1), 0)
    qpos = t0 + lax.broadcasted_iota(jnp.int32, (1, tq), 1)
    causal = kpos <= qpos
    key = key_s[...]
    thr = thr_s[0:1, :]
    gt = (key > thr) & causal
    eq = (key == thr) & causal
    keep = topk - _count(gt)
    nbits = seq.bit_length()

    def cut_body(i, cut):
        cand = cut | jnp.left_shift(jnp.int32(1), nbits - 1 - i)
        ok = _count(eq & (kpos < cand)) <= keep
        return jnp.where(ok, cand, cut)
    cut = lax.fori_loop(0, nbits, cut_body, jnp.zeros((1, tq), jnp.int32))
    bias_s[...] = jnp.where(gt | (eq & (kpos < cut)), 0.0, -jnp.inf)


def _attn_heads(c, ckv_ref, ckvt_ref, ql_s, bias_s, ol_s, lg_s, m_s, d_s, e_s, *, n_heads):
    tq = bias_s.shape[1]
    s = (c + 1) * tq

    def logits(q, b):
        lg = _dot(ckv_ref[:s, :], q) + bias_s[:s, :]
        lg_s[b, :s, :] = lg
        m_s[b] = jnp.broadcast_to(_reduce_rows(jnp.max, lg), (SUBLANES, tq))

    def softmax(b):
        e = jnp.exp2(lg_s[b, :s, :] - m_s[b][0:1, :])
        d_s[b] = jnp.broadcast_to(_reduce_rows(jnp.sum, e), (SUBLANES, tq))
        e_s[b, :s, :] = e.astype(BF16)

    def pv(b):
        return (_dot(ckvt_ref[:, :s], e_s[b, :s, :]) / d_s[b][0:1, :]).astype(BF16)

    def steps(t0, count):
        qs = [ql_s[t0 + u] for u in range(count)]
        outs = []
        for u in range(count):
            outs.append(pv(u % 2))
            softmax((u + 1) % 2)
            logits(qs[u], u % 2)
        for u in range(count):
            ol_s[t0 + u - 2] = outs[u]

    logits(ql_s[0], 0)
    softmax(0)
    logits(ql_s[1], 1)

    n_steady = n_heads - 2
    peeled = n_steady % PIPE_UNROLL
    if peeled:
        steps(2, peeled)
    if n_steady // PIPE_UNROLL > 1:
        def pipe_body(p, carry):
            steps(2 + peeled + PIPE_UNROLL * p, PIPE_UNROLL)
            return carry
        lax.fori_loop(0, n_steady // PIPE_UNROLL, pipe_body, 0)
    elif n_steady // PIPE_UNROLL == 1:
        steps(2 + peeled, PIPE_UNROLL)

    oa = pv(0)
    softmax(1)
    ol_s[n_heads - 2] = oa
    ol_s[n_heads - 1] = pv(1)


def _attn_kernel(cqt_ref, widxt_ref, kidx_ref, ckv_ref, ckvt_ref,
                 wuqt_ref, wukt_ref, wqit_ref, wuvt_ref, o_ref,
                 ql_s, qi_s, w_s, key_s, bias_s, ol_s, lg_s, m_s, d_s, e_s, thr_s, tie_s,
                 *, n_heads, topk, qscale):
    seq, tq = key_s.shape
    hd = wukt_ref.shape[2]
    vd = wuvt_ref.shape[1]
    cqt = cqt_ref[...]
    q_all = _dot(wuqt_ref[...], cqt).astype(BF16)
    qi_all = _dot(wqit_ref[...], cqt).astype(BF16)
    widxt = widxt_ref[...]
    for h in range(n_heads):
        q_lat = _dot(wukt_ref[h], q_all[h * hd:(h + 1) * hd, :]) * qscale
        ql_s[h] = q_lat.astype(BF16)
        qi_s[h] = qi_all[h * LANES:(h + 1) * LANES, :]
        w_s[h] = jnp.broadcast_to(widxt[h:h + 1, :], (SUBLANES, tq))

    qtile = pl.program_id(1)
    for c in range(seq // tq):
        pl.when(qtile == c)(functools.partial(
            _attn_select, c, kidx_ref, qi_s, w_s, key_s, bias_s, thr_s, tie_s,
            n_heads=n_heads, topk=topk))
    pl.when(tie_s[0] != 0)(functools.partial(
        _attn_ties, qtile * tq, key_s, bias_s, thr_s, topk=topk))
    for c in range(seq // tq):
        pl.when(qtile == c)(functools.partial(
            _attn_heads, c, ckv_ref, ckvt_ref, ql_s, bias_s, ol_s, lg_s, m_s, d_s, e_s,
            n_heads=n_heads))

    for h in range(n_heads):
        o_ref[:, h * vd:(h + 1) * vd] = _dot(wuvt_ref[h], ol_s[h]).T.astype(BF16)


def _attn(cqt, widxt, kidx, ckv, ckvt, wuqt, wukt, wqit, wuvt, *, batch, seq, topk, scale, tq):
    q_lora, n = cqt.shape
    n_heads, kv_lora, hd = wukt.shape
    vd = wuvt.shape[1]
    nq = seq // tq
    const2 = lambda a: pl.BlockSpec(a.shape, lambda b, q: (0, 0))
    const3 = lambda a: pl.BlockSpec(a.shape, lambda b, q: (0, 0, 0))
    return pl.pallas_call(
        functools.partial(_attn_kernel, n_heads=n_heads, topk=topk,
                          qscale=scale * LOG2_E),
        grid=(batch, nq),
        in_specs=[
            pl.BlockSpec((q_lora, tq), lambda b, q: (0, b * nq + q)),
            pl.BlockSpec((LANES, tq), lambda b, q: (0, b * nq + q)),
            pl.BlockSpec((seq, LANES), lambda b, q: (b, 0)),
            pl.BlockSpec((seq, kv_lora), lambda b, q: (b, 0)),
            pl.BlockSpec((kv_lora, seq), lambda b, q: (0, b)),
            const2(wuqt), const3(wukt), const2(wqit), const3(wuvt),
        ],
        out_specs=pl.BlockSpec((tq, n_heads * vd), lambda b, q: (b * nq + q, 0)),
        out_shape=jax.ShapeDtypeStruct((n, n_heads * vd), BF16),
        scratch_shapes=[
            pltpu.VMEM((n_heads, kv_lora, tq), BF16),
            pltpu.VMEM((n_heads, LANES, tq), BF16),
            pltpu.VMEM((n_heads, SUBLANES, tq), F32),
            pltpu.VMEM((seq, tq), jnp.int32),
            pltpu.VMEM((seq, tq), F32),
            pltpu.VMEM((n_heads, kv_lora, tq), BF16),
            pltpu.VMEM((2, seq, tq), F32),
            pltpu.VMEM((2, SUBLANES, tq), F32),
            pltpu.VMEM((2, SUBLANES, tq), F32),
            pltpu.VMEM((2, seq, tq), BF16),
            pltpu.VMEM((SUBLANES, tq), jnp.int32),
            pltpu.SMEM((1,), jnp.int32),
        ],
        compiler_params=_params(2),
        name="dsa_attention",
    )(cqt, widxt, kidx, ckv, ckvt, wuqt, wukt, wqit, wuvt)


def _out_proj_kernel(h_ref, o_ref, w_ref, out_ref):
    out_ref[...] = h_ref[...] + _dot(o_ref[...], w_ref[...])


def _out_proj(h, o, w, *, layer, tm):
    n, d = h.shape
    k = o.shape[1]
    return pl.pallas_call(
        _out_proj_kernel,
        grid=(n // tm,),
        in_specs=[
            pl.BlockSpec((tm, d), lambda i: (i, 0)),
            pl.BlockSpec((tm, k), lambda i: (i, 0)),
            pl.BlockSpec((None, k, d), lambda i: (layer, 0, 0)),
        ],
        out_specs=pl.BlockSpec((tm, d), lambda i: (i, 0)),
        out_shape=jax.ShapeDtypeStruct((n, d), F32),
        compiler_params=_params(1),
        name="attn_out_proj",
    )(h, o, w)


def _pad_last(a, width):
    return jnp.pad(a, [(0, 0)] * (a.ndim - 1) + [(0, width - a.shape[-1])])


def _tile(n, want):
    t = min(n, want)
    assert n % t == 0, (n, t)
    return t


def kernel(x, norm_mix, norm_mlp, mlp_w1, mlp_w2, conv_in, conv_w, conv_out, attn_in,
           q_norm, kv_norm, w_uq, w_uk, w_uv, w_qidx, kidx_ln_g, kidx_ln_b, attn_out,
           final_norm):
    batch, seq, d = x.shape
    depth = norm_mix.shape[0]
    q_lora, n_heads, hd = w_uq.shape[1:]
    kv_lora = w_uk.shape[3]
    idx_heads, idx_dim = w_qidx.shape[2:]
    assert idx_heads == n_heads and idx_dim <= LANES and n_heads <= LANES
    assert n_heads % SCORE_GROUP == 0 and n_heads % 2 == 0 and n_heads >= 4
    assert conv_w.shape[1] == CONV_WIDTH
    topk = min(TOPK_MAX, seq // 4)
    n = batch * seq

    tm = _tile(seq, 512)
    tf = _tile(mlp_w1.shape[2], 2048)
    tc = _tile(d, 512)
    tq = _tile(seq, 256)

    mlp_w1, mlp_w2, conv_in, conv_out, attn_out = (
        w.astype(BF16) for w in (mlp_w1, mlp_w2, conv_in, conv_out, attn_out))

    h = x.reshape(n, d)
    row = lambda v: v.reshape(1, -1)
    gf = row(final_norm)
    for i in range(depth):
        j = i // 2
        g = row(norm_mix[i])
        if i % 2 == 0:
            cw = jnp.pad(conv_w[j], ((0, SUBLANES - CONV_WIDTH), (0, 0)))
            h = _conv(h, g, conv_in, cw, conv_out, layer=j, seq=seq, tm=tm, tc=tc)
        else:
            o = q_lora + kv_lora
            w = jnp.concatenate([
                attn_in[j][:, :o],
                _pad_last(attn_in[j][:, o:o + idx_dim], LANES),
                _pad_last(attn_in[j][:, o + idx_dim:], LANES)], axis=1).astype(BF16)
            cqt, ckv, ckvt, kidx, widxt = _attn_proj(
                h, g, w, row(q_norm[j]), row(kv_norm[j]),
                _pad_last(row(kidx_ln_g[j]), LANES), _pad_last(row(kidx_ln_b[j]), LANES),
                q_lora=q_lora, kv_lora=kv_lora, idx_dim=idx_dim,
                w_scale=float(idx_heads ** -0.5 * idx_dim ** -0.5), tm=tm)
            wuqt = w_uq[j].reshape(q_lora, n_heads * hd).T.astype(BF16)
            wukt = w_uk[j].transpose(0, 2, 1).astype(BF16)
            wqit = _pad_last(w_qidx[j], LANES).reshape(q_lora, n_heads * LANES).T.astype(BF16)
            wuvt = w_uv[j].transpose(0, 2, 1).astype(BF16)
            oh = _attn(cqt, widxt, kidx, ckv, ckvt, wuqt, wukt, wqit, wuvt,
                       batch=batch, seq=seq, topk=topk, scale=float(hd ** -0.5), tq=tq)
            h = _out_proj(h, oh, attn_out, layer=j, tm=tm)
        h = _mlp(h, row(norm_mlp[i]), mlp_w1, mlp_w2, gf,
                 layer=i, final=(i == depth - 1), tm=tm, tf=tf)
    return h.reshape(batch, seq, d)
```
